```python
import math
import jax, jax.numpy as jnp
from jax import lax
import numpy as np

D_MODEL = 4096
BATCH = 4
SEQ = 4096
DEPTH = 2

CHUNK = 64
N_META = 16
Q_BLOCK = 128
MIX_WIDTH = D_MODEL
A_WIDTH = MIX_WIDTH // 2
B_WIDTH = MIX_WIDTH - A_WIDTH
A_DK = 128
A_HEADS = A_WIDTH // A_DK
A_DV = A_WIDTH // A_HEADS
B_DV = 128
B_HEADS = B_WIDTH // B_DV
B_DH = B_DV // 2
D_FF = 4 * D_MODEL
IN_COLS = 4 * A_WIDTH + 2 * (B_HEADS * 2 * B_DH) + B_HEADS * B_DV
EPS = 1e-6
MASK_VALUE = -1e30

kernel_name = "hymba_hgrn2_diffattn_chunk_causal"


def rms_norm(x, g):
    xf = x.astype(jnp.float32)
    y = xf * lax.rsqrt(jnp.mean(xf * xf, axis=-1, keepdims=True) + EPS)
    return (y * g.astype(jnp.float32)).astype(x.dtype)


def chunk_id(pos):
    return jnp.where(pos < N_META, 0, (pos - N_META) // CHUNK + 1)


def alibi_slopes(n_heads):
    h = jnp.arange(1, n_heads + 1, dtype=jnp.float32)
    return jnp.exp2(-8.0 * h / n_heads)


def hgrn2_chunk_step(S, inp):
    q, k, v, logf = inp
    C = q.shape[1]
    b = jnp.cumsum(logf, axis=1)
    causal = jnp.tril(jnp.ones((C, C), dtype=bool))
    diff = b[:, :, None] - b[:, None, :]
    decay = jnp.exp(jnp.where(causal[None, :, :, None, None], diff, -jnp.inf))
    A = jnp.einsum('bthd,bshd,btshd->bhts', q, k, decay)
    o_intra = jnp.einsum('bhts,bshe->bthe', A, v)
    o_inter = jnp.einsum('bthd,bhde->bthe', q * jnp.exp(b), S)
    b_last = b[:, -1]
    k_dec = k * jnp.exp(b_last[:, None] - b)
    S_new = jnp.exp(b_last)[..., None] * S + jnp.einsum('bshd,bshe->bhde', k_dec, v)
    return S_new, o_intra + o_inter


def hgrn2_mixer(q, fpre, i, g, lb, out_g):
    Bn, L, _ = q.shape
    f32 = jnp.float32
    qh = q.astype(f32).reshape(Bn, L, A_HEADS, A_DK)
    fh = fpre.astype(f32).reshape(Bn, L, A_HEADS, A_DK)
    vh = i.astype(f32).reshape(Bn, L, A_HEADS, A_DV)
    lbh = lb.reshape(A_HEADS, A_DK)
    f = lbh + (1.0 - lbh) * jax.nn.sigmoid(fh)
    logf = jnp.log(f)
    kh = (1.0 - lbh) * jax.nn.sigmoid(-fh)
    S0 = jnp.zeros((Bn, A_HEADS, A_DK, A_DV), f32)
    S1, o_meta = hgrn2_chunk_step(S0, (qh[:, :N_META], kh[:, :N_META], vh[:, :N_META], logf[:, :N_META]))
    n_chunks = (L - N_META) // CHUNK

    def to_chunks(t):
        t = t[:, N_META:]
        t = t.reshape(Bn, n_chunks, CHUNK, *t.shape[2:])
        return jnp.moveaxis(t, 1, 0)

    _, o_real = lax.scan(hgrn2_chunk_step, S1, (to_chunks(qh), to_chunks(kh), to_chunks(vh), to_chunks(logf)))
    o_real = jnp.moveaxis(o_real, 0, 1).reshape(Bn, L - N_META, A_HEADS, A_DV)
    o = jnp.concatenate([o_meta, o_real], axis=1)
    o = rms_norm(o, out_g).reshape(Bn, L, A_WIDTH)
    return (o * jax.nn.silu(g.astype(f32))).astype(q.dtype)


def diff_attention(q, k, v, lam, lam_init, out_g):
    Bn, L = q.shape[0], q.shape[1]
    pos = jnp.arange(L)
    kchunk = chunk_id(pos)
    slopes = alibi_slopes(B_HEADS)
    scale = 1.0 / math.sqrt(B_DH)

    def attend(qb, qposb):
        s = jnp.einsum('bqhcd,bkhcd->bchqk', qb, k).astype(jnp.float32) * scale
        dist = jnp.abs(qposb[:, None] - pos[None, :]).astype(jnp.float32)
        s = s - slopes[:, None, None] * dist
        mask = kchunk[None, :] <= chunk_id(qposb)[:, None]
        s = jnp.where(mask, s, MASK_VALUE)
        p = jax.nn.softmax(s, axis=-1)
        a = p[:, 0] - lam * p[:, 1]
        return jnp.einsum('bhqk,bkhe->bqhe', a.astype(v.dtype), v)

    o_meta = attend(q[:, :N_META], pos[:N_META])
    n_blk = (L - N_META) // Q_BLOCK
    qr = jnp.moveaxis(q[:, N_META:].reshape(Bn, n_blk, Q_BLOCK, B_HEADS, 2, B_DH), 1, 0)
    pr = pos[N_META:].reshape(n_blk, Q_BLOCK)
    o_real = lax.map(lambda a: attend(a[0], a[1]), (qr, pr))
    o_real = jnp.moveaxis(o_real, 0, 1).reshape(Bn, L - N_META, B_HEADS, B_DV)
    o = jnp.concatenate([o_meta, o_real], axis=1)
    o = rms_norm(o, out_g) * (1.0 - lam_init)
    return o.reshape(Bn, L, B_WIDTH).astype(q.dtype)


def setup_inputs(seed: int = 0) -> dict:
    key = jax.random.key(seed)
    ks = jax.random.split(key, 16)
    f32 = jnp.float32
    nrm = lambda k, shape, std: jax.random.normal(k, shape, f32) * std
    return {
        "x": nrm(ks[0], (BATCH, SEQ, D_MODEL), 1.0),
        "meta_tokens": nrm(ks[1], (N_META, D_MODEL), 1.0),
        "norm1_g": 1.0 + nrm(ks[2], (DEPTH, D_MODEL), 0.01),
        "w_in": nrm(ks[3], (DEPTH, D_MODEL, IN_COLS), D_MODEL ** -0.5),
        "hgrn_lb_raw": nrm(ks[4], (DEPTH, A_WIDTH), 0.1),
        "hgrn_out_g": 1.0 + nrm(ks[5], (DEPTH, A_DV), 0.01),
        "q_norm_g": 1.0 + nrm(ks[6], (DEPTH, 2, B_DH), 0.01),
        "k_norm_g": 1.0 + nrm(ks[7], (DEPTH, 2, B_DH), 0.01),
        "diff_lambda": nrm(ks[8], (DEPTH, 4, B_DH), 0.1),
        "diff_out_g": 1.0 + nrm(ks[9], (DEPTH, B_DV), 0.01),
        "w_out": nrm(ks[10], (DEPTH, MIX_WIDTH, D_MODEL), MIX_WIDTH ** -0.5),
        "norm2_g": 1.0 + nrm(ks[11], (DEPTH, D_MODEL), 0.01),
        "w_mlp_up": nrm(ks[12], (DEPTH, D_MODEL, D_FF), D_MODEL ** -0.5),
        "w_mlp_down": nrm(ks[13], (DEPTH, D_FF, D_MODEL), 0.5 * D_FF ** -0.5),
    }


def reference(x, meta_tokens, norm1_g, w_in, hgrn_lb_raw, hgrn_out_g, q_norm_g, k_norm_g,
              diff_lambda, diff_out_g, w_out, norm2_g, w_mlp_up, w_mlp_down):
    Bn = x.shape[0]
    meta = jnp.broadcast_to(meta_tokens.astype(x.dtype)[None], (Bn, N_META, D_MODEL))
    h = jnp.concatenate([meta, x], axis=1)
    L = h.shape[1]
    lb_all = jnp.cumsum(jax.nn.softmax(hgrn_lb_raw.astype(jnp.float32), axis=0), axis=0)
    lb_all = lb_all - lb_all[0:1]
    split_at = np.cumsum([A_WIDTH, A_WIDTH, A_WIDTH, A_WIDTH,
                          B_HEADS * 2 * B_DH, B_HEADS * 2 * B_DH])
    for layer in range(DEPTH):
        u = rms_norm(h, norm1_g[layer])
        proj = u @ w_in[layer]
        a_q, a_f, a_i, a_g, b_q, b_k, b_v = jnp.split(proj, split_at, axis=-1)
        o_a = hgrn2_mixer(a_q, a_f, a_i, a_g, lb_all[layer], hgrn_out_g[layer])
        bq = rms_norm(b_q.reshape(Bn, L, B_HEADS, 2, B_DH), q_norm_g[layer])
        bk = rms_norm(b_k.reshape(Bn, L, B_HEADS, 2, B_DH), k_norm_g[layer])
        bv = b_v.reshape(Bn, L, B_HEADS, B_DV)
        lp = diff_lambda[layer].astype(jnp.float32)
        lam_init = 0.8 - 0.6 * math.exp(-0.3 * layer)
        lam = jnp.exp(jnp.sum(lp[0] * lp[1])) - jnp.exp(jnp.sum(lp[2] * lp[3])) + lam_init
        o_b = diff_attention(bq, bk, bv, lam, lam_init, diff_out_g[layer])
        h = h + jnp.concatenate([o_a, o_b], axis=-1) @ w_out[layer]
        u = rms_norm(h, norm2_g[layer])
        z = jax.nn.relu(u @ w_mlp_up[layer])
        h = h + (z * z) @ w_mlp_down[layer]
    return h[:, N_META:]
```

```python
import functools
import math

import numpy as np
import jax
import jax.numpy as jnp
from jax import lax
from jax.experimental import pallas as pl
from jax.experimental.pallas import tpu as pltpu

N_META = 16
ATTN_CHUNK = 64
HEAD = 128
HALF = HEAD // 2
ROW_TILE = 128
PAD = ROW_TILE - N_META
HGRN_CHUNK = 128
HGRN_LEVELS = 7
EPS = 1e-6
MASK_VALUE = -1e30
V7X_VMEM_LIMIT_BYTES = 56 * 1024 * 1024

F32 = jnp.float32
BF16 = jnp.bfloat16


def _params(semantics):
    return pltpu.CompilerParams(dimension_semantics=semantics, vmem_limit_bytes=V7X_VMEM_LIMIT_BYTES)


def _pick(n, candidates):
    for c in candidates:
        if n % c == 0:
            return c
    raise ValueError(f"no tile for {n} among {candidates}")


def _rmsnorm_kernel(x_ref, g_ref, o_ref):
    x = x_ref[...]
    ms = jnp.mean(x * x, axis=-1, keepdims=True)
    o_ref[...] = (x * lax.rsqrt(ms + EPS) * g_ref[...]).astype(o_ref.dtype)


def rmsnorm(x, g):
    m, d = x.shape
    bm = _pick(m, (384, 256, 128))
    return pl.pallas_call(
        _rmsnorm_kernel,
        grid=(m // bm,),
        in_specs=[pl.BlockSpec((bm, d), lambda i: (i, 0)), pl.BlockSpec((1, d), lambda i: (0, 0))],
        out_specs=pl.BlockSpec((bm, d), lambda i: (i, 0)),
        out_shape=jax.ShapeDtypeStruct((m, d), BF16),
        compiler_params=_params(("parallel",)),
        name="rmsnorm",
    )(x, g.reshape(1, d).astype(F32))


def _matmul_kernel(*refs, n_parts, epilogue):
    xs = refs[:n_parts]
    ws = refs[n_parts:2 * n_parts]
    rest = refs[2 * n_parts:]
    acc = jnp.dot(xs[0][...], ws[0][...], preferred_element_type=F32)
    for x_ref, w_ref in zip(xs[1:], ws[1:]):
        acc = acc + jnp.dot(x_ref[...], w_ref[...], preferred_element_type=F32)
    if epilogue == "relu2":
        acc = jnp.square(jnp.maximum(acc, 0.0))
    if epilogue == "residual":
        r_ref, o_ref = rest
        acc = acc + r_ref[...]
    else:
        (o_ref,) = rest
    o_ref[...] = acc.astype(o_ref.dtype)


def matmul(xs, ws, out_dtype, epilogue=None, residual=None):
    m = xs[0].shape[0]
    n = ws[0].shape[1]
    bm = _pick(m, (1056, 768, 512, 256, 128))
    bn = _pick(n, (512, 256, 128))
    in_specs = [pl.BlockSpec((bm, x.shape[1]), lambda i, j: (i, 0)) for x in xs]
    in_specs += [pl.BlockSpec((w.shape[0], bn), lambda i, j: (0, j)) for w in ws]
    args = list(xs) + list(ws)
    if epilogue == "residual":
        in_specs.append(pl.BlockSpec((bm, bn), lambda i, j: (i, j)))
        args.append(residual)
    return pl.pallas_call(
        functools.partial(_matmul_kernel, n_parts=len(xs), epilogue=epilogue),
        grid=(m // bm, n // bn),
        in_specs=in_specs,
        out_specs=pl.BlockSpec((bm, bn), lambda i, j: (i, j)),
        out_shape=jax.ShapeDtypeStruct((m, n), out_dtype),
        compiler_params=_params(("parallel", "parallel")),
        name="matmul_" + (epilogue or "plain"),
    )(*args)


def _matmul_kgrid_kernel(x_ref, w_ref, r_ref, o_ref, acc_ref):
    k = pl.program_id(2)

    @pl.when(k == 0)
    def _():
        acc_ref[...] = r_ref[...]

    acc_ref[...] += jnp.dot(x_ref[...], w_ref[...], preferred_element_type=F32)

    @pl.when(k == pl.num_programs(2) - 1)
    def _():
        o_ref[...] = acc_ref[...]


def matmul_residual_kgrid(x, w, residual):
    m, kdim = x.shape
    n = w.shape[1]
    bm = _pick(m, (1056, 768, 512, 256, 128))
    bn = _pick(n, (1024, 512, 256, 128))
    bk = _pick(kdim, (2048, 1024, 512, 256, 128))
    return pl.pallas_call(
        _matmul_kgrid_kernel,
        grid=(m // bm, n // bn, kdim // bk),
        in_specs=[
            pl.BlockSpec((bm, bk), lambda i, j, k: (i, k)),
            pl.BlockSpec((bk, bn), lambda i, j, k: (k, j)),
            pl.BlockSpec((bm, bn), lambda i, j, k: (i, j)),
        ],
        out_specs=pl.BlockSpec((bm, bn), lambda i, j, k: (i, j)),
        out_shape=jax.ShapeDtypeStruct((m, n), F32),
        scratch_shapes=[pltpu.VMEM((bm, bn), F32)],
        compiler_params=_params(("parallel", "parallel", "arbitrary")),
        name="matmul_residual_kgrid",
    )(x, w, residual)


def _qkv_prep_kernel(q_ref, k_ref, v_ref, gq_ref, gk_ref, pool_ref, qo_ref, ko_ref, vo_ref):
    pool = pool_ref[...]
    width = q_ref.shape[1]
    for src, g_ref, dst in ((q_ref, gq_ref, qo_ref), (k_ref, gk_ref, ko_ref)):
        for h in range(width // HEAD):
            cols = slice(h * HEAD, (h + 1) * HEAD)
            x = src[:, cols]
            ms = jnp.dot((x * x).astype(BF16), pool, preferred_element_type=F32)
            dst[:, cols] = (x * lax.rsqrt(ms + EPS) * g_ref[...]).astype(dst.dtype)
    vo_ref[...] = v_ref[...].astype(vo_ref.dtype)


def qkv_prep(proj, q_gain, k_gain, width):
    m = proj.shape[0]
    bm = _pick(m, (384, 256, 128))
    seg = (4 * width) // width
    pool = np.kron(np.eye(2), np.full((HALF, HALF), 1.0 / HALF)).astype(np.float32)
    out = jax.ShapeDtypeStruct((m, width), BF16)
    gspec = pl.BlockSpec((1, HEAD), lambda i: (0, 0))
    return pl.pallas_call(
        _qkv_prep_kernel,
        grid=(m // bm,),
        in_specs=[
            pl.BlockSpec((bm, width), lambda i: (i, seg)),
            pl.BlockSpec((bm, width), lambda i: (i, seg + 1)),
            pl.BlockSpec((bm, width), lambda i: (i, seg + 2)),
            gspec, gspec,
            pl.BlockSpec((HEAD, HEAD), lambda i: (0, 0)),
        ],
        out_specs=[pl.BlockSpec((bm, width), lambda i: (i, 0))] * 3,
        out_shape=[out, out, out],
        compiler_params=_params(("parallel",)),
        name="qkv_prep",
    )(proj, proj, proj, q_gain, k_gain, jnp.asarray(pool, BF16))


def _attn_tile(length):
    return _pick(length, (384, 256, 128))


def _attn_bias_tiles(n_heads, tile):
    slopes = np.exp2(-8.0 * np.arange(1, n_heads + 1, dtype=np.float64) / n_heads)
    ki = np.arange(tile)[:, None]
    qi = np.arange(tile)[None, :]
    past = (ki - qi).astype(np.float64)
    allowed = (ki // ATTN_CHUNK) <= (qi // ATTN_CHUNK)
    diag = np.where(allowed, -np.abs(qi - ki).astype(np.float64), 0.0)
    key_is_pad = np.broadcast_to(ki < PAD, (tile, tile))
    tiles = []
    for s in slopes:
        p = s * past
        d = np.where(allowed, s * diag, MASK_VALUE)
        tiles.append(np.stack([p, d, np.where(key_is_pad, MASK_VALUE, p), np.where(key_is_pad, MASK_VALUE, d)]))
    return jnp.asarray(np.stack(tiles).astype(np.float32)), jnp.asarray(slopes.astype(np.float32))


def _attn_kernel(scal_ref, slope_ref, q_ref, k_ref, v_ref, bias_ref, g_ref, o_ref, acc_ref, *, tile):
    h = pl.program_id(0)
    i = pl.program_id(2)
    lam = scal_ref[0]
    out_scale = scal_ref[1]
    block_shift = slope_ref[h] * float(tile)

    q = q_ref[0]
    lane = lax.broadcasted_iota(jnp.int32, q.shape, 1)
    qmaps = (jnp.where(lane < HALF, q, jnp.zeros_like(q)), jnp.where(lane >= HALF, q, jnp.zeros_like(q)))

    acc_ref[...] = jnp.zeros_like(acc_ref)
    init = tuple(jnp.full((1, tile), MASK_VALUE, F32) if n % 2 == 0 else jnp.zeros((1, tile), F32)
                 for n in range(4))

    def step(j, which, carry):
        start = pl.multiple_of(j * tile, tile)
        kb = k_ref[0, pl.ds(start, tile), :]
        vb = v_ref[0, pl.ds(start, tile), :]
        bias = bias_ref[0, which]
        shift = block_shift * (j - i).astype(F32)
        new = []
        for c in range(2):
            m_old, l_old = carry[2 * c], carry[2 * c + 1]
            s = lax.dot_general(kb, qmaps[c], (((1,), (1,)), ((), ())), preferred_element_type=F32) + bias
            m_new = jnp.maximum(m_old, jnp.max(s, axis=0, keepdims=True) + shift)
            p = jnp.exp(s - (m_new - shift))
            alpha = jnp.exp(m_old - m_new)
            l_new = alpha * l_old + jnp.sum(p, axis=0, keepdims=True)
            pv = lax.dot_general(vb, p.astype(BF16), (((0,), (0,)), ((), ())), preferred_element_type=F32)
            acc_ref[c] = acc_ref[c] * alpha + pv
            new += [m_new, l_new]
        return tuple(new)

    zero = jnp.int32(0)
    carry = lax.cond(i > 0, lambda c: step(zero, 2, c), lambda c: c, init)
    carry = lax.fori_loop(1, i, lambda j, c: step(j, 0, c), carry)
    carry = lax.cond(i > 0, lambda c: step(i, 1, c), lambda c: step(i, 3, c), carry)

    o = acc_ref[0] / carry[1] - lam * (acc_ref[1] / carry[3])
    o = o * lax.rsqrt(jnp.mean(o * o, axis=0, keepdims=True) + EPS)
    ot = o.T * (g_ref[...] * out_scale)
    row = lax.broadcasted_iota(jnp.int32, ot.shape, 0) + i * tile
    o_ref[0] = jnp.where(row >= PAD, ot, 0.0).astype(o_ref.dtype)


def diff_attention(qn, kn, vb, lam, lam_init, out_g, batch, length):
    width = qn.shape[1]
    n_heads = width // HEAD
    tile = _attn_tile(length)
    nq = length // tile
    bias, slopes = _attn_bias_tiles(n_heads, tile)
    scal = jnp.stack([lam.astype(F32), jnp.asarray(1.0 - lam_init, F32)])
    q3, k3, v3 = (a.reshape(batch, length, width) for a in (qn, kn, vb))
    smem = pl.BlockSpec(memory_space=pltpu.SMEM)
    out = pl.pallas_call(
        functools.partial(_attn_kernel, tile=tile),
        grid=(n_heads, batch, nq),
        in_specs=[
            smem, smem,
            pl.BlockSpec((1, tile, HEAD), lambda h, b, i: (b, i, h)),
            pl.BlockSpec((1, length, HEAD), lambda h, b, i: (b, 0, h)),
            pl.BlockSpec((1, length, HEAD), lambda h, b, i: (b, 0, h)),
            pl.BlockSpec((1, 4, tile, tile), lambda h, b, i: (h, 0, 0, 0)),
            pl.BlockSpec((1, HEAD), lambda h, b, i: (0, 0)),
        ],
        out_specs=pl.BlockSpec((1, tile, HEAD), lambda h, b, i: (b, i, h)),
        out_shape=jax.ShapeDtypeStruct((batch, length, width), BF16),
        scratch_shapes=[pltpu.VMEM((2, HEAD, tile), F32)],
        compiler_params=_params(("parallel", "parallel", "parallel")),
        name="diff_attention",
    )(scal, slopes, q3, k3, v3, bias, out_g.reshape(1, HEAD).astype(F32))
    return out.reshape(batch * length, width)


def _hgrn_constants():
    c = HGRN_CHUNK
    t = np.arange(c)[:, None]
    r = np.arange(c)[None, :]
    sums = []
    pair = [np.eye(c)]
    for level in range(1, HGRN_LEVELS + 1):
        size = 2 ** level
        mid = (t // size) * size + size // 2 - 1
        upper = (t % size) >= size // 2
        sums.append(np.where(upper, (r > mid) & (r <= t), (r > t) & (r <= mid)))
        same = (t // size) == (r // size)
        pair.append(same & upper & ((r % size) < size // 2))
    sums.append(r <= t)
    sums.append(r > t)
    sums = np.concatenate(sums, axis=0).astype(np.float32)
    return (jnp.asarray(np.concatenate([sums, sums], axis=1), BF16),
            jnp.asarray(np.stack(pair).astype(np.float32)))


def _hgrn_kernel(q_ref, f_ref, v_ref, g_ref, lb_ref, og_ref, sums_ref, pair_ref, o_ref, state_ref,
                 *, heads, chunks):
    c = HGRN_CHUNK

    @pl.when(pl.program_id(2) == 0)
    def _():
        state_ref[...] = jnp.zeros_like(state_ref)

    def chunk_body(n, _):
        rows = pl.ds(pl.multiple_of(n * c, c), c)
        for h in range(heads):
            cols = slice(h * HEAD, (h + 1) * HEAD)
            q = q_ref[0, rows, cols]
            fpre = f_ref[0, rows, cols]
            v = v_ref[0, rows, cols].astype(BF16)
            gate = g_ref[0, rows, cols]
            lb = lb_ref[:, cols]
            sig = jax.nn.sigmoid(fpre)
            f = lb + (1.0 - lb) * sig
            k = (1.0 - lb) * (1.0 - sig)
            logf = jnp.log(f)
            hi = logf.astype(BF16)
            lo = (logf - hi.astype(F32)).astype(BF16)
            logd = jnp.dot(sums_ref[...], jnp.concatenate([hi, lo], axis=0),
                           preferred_element_type=F32)
            decay = jnp.exp(logd)

            a = pair_ref[0] * lax.dot_general(q.astype(BF16), k.astype(BF16), (((1,), (1,)), ((), ())),
                                              preferred_element_type=F32)
            for level in range(1, HGRN_LEVELS + 1):
                d = decay[(level - 1) * c:level * c]
                a = a + pair_ref[level] * lax.dot_general(
                    (q * d).astype(BF16), (k * d).astype(BF16), (((1,), (1,)), ((), ())),
                    preferred_element_type=F32)
            from_start = decay[7 * c:8 * c]
            to_end = decay[8 * c:9 * c]
            state_t = state_ref[h]
            o = jnp.dot(a.astype(BF16), v, preferred_element_type=F32)
            o = o + lax.dot_general((q * from_start).astype(BF16), state_t.astype(BF16),
                                    (((1,), (1,)), ((), ())), preferred_element_type=F32)
            state_ref[h] = state_t * from_start[c - 1:c, :] + lax.dot_general(
                v, (k * to_end).astype(BF16), (((0,), (0,)), ((), ())), preferred_element_type=F32)

            y = o * lax.rsqrt(jnp.mean(o * o, axis=-1, keepdims=True) + EPS) * og_ref[...]
            o_ref[0, rows, cols] = (y * (gate * jax.nn.sigmoid(gate))).astype(o_ref.dtype)
        return 0

    lax.fori_loop(0, chunks, chunk_body, 0)


def hgrn2(proj, lb, out_g, batch, length, width):
    n_heads = width // HEAD
    heads = _pick(n_heads, (4, 2, 1))
    tile = _pick(length, (384, 256, 128))
    groups = n_heads // heads
    bw = heads * HEAD
    sums, pair = _hgrn_constants()
    proj3 = proj.reshape(batch, length, proj.shape[1])

    def seg_spec(seg):
        return pl.BlockSpec((1, tile, bw), lambda b, hg, t: (b, t, seg * groups + hg))

    out = pl.pallas_call(
        functools.partial(_hgrn_kernel, heads=heads, chunks=tile // HGRN_CHUNK),
        grid=(batch, groups, length // tile),
        in_specs=[
            seg_spec(0), seg_spec(1), seg_spec(2), seg_spec(3),
            pl.BlockSpec((1, bw), lambda b, hg, t: (0, hg)),
            pl.BlockSpec((1, HEAD), lambda b, hg, t: (0, 0)),
            pl.BlockSpec(sums.shape, lambda b, hg, t: (0, 0)),
            pl.BlockSpec(pair.shape, lambda b, hg, t: (0, 0, 0)),
        ],
        out_specs=pl.BlockSpec((1, tile, bw), lambda b, hg, t: (b, t, hg)),
        out_shape=jax.ShapeDtypeStruct((batch, length, width), BF16),
        scratch_shapes=[pltpu.VMEM((heads, HEAD, HEAD), F32)],
        compiler_params=_params(("parallel", "parallel", "arbitrary")),
        name="hgrn2",
    )(proj3, proj3, proj3, proj3, lb.reshape(1, width).astype(F32), out_g.reshape(1, HEAD).astype(F32),
      sums, pair)
    return out.reshape(batch * length, width)


def kernel(x, meta_tokens, norm1_g, w_in, hgrn_lb_raw, hgrn_out_g, q_norm_g, k_norm_g, diff_lambda,
           diff_out_g, w_out, norm2_g, w_mlp_up, w_mlp_down):
    batch, seq, d_model = x.shape
    depth = w_in.shape[0]
    width = d_model // 2
    length = PAD + N_META + seq
    meta = jnp.broadcast_to(meta_tokens.astype(x.dtype)[None], (batch, N_META, d_model))
    h = jnp.concatenate([jnp.zeros((batch, PAD, d_model), x.dtype), meta, x], axis=1)
    h = h.reshape(batch * length, d_model)

    lb_all = jnp.cumsum(jax.nn.softmax(hgrn_lb_raw.astype(F32), axis=0), axis=0)
    lb_all = lb_all - lb_all[0:1]
    qk_scale = 1.0 / math.sqrt(HALF)

    for layer in range(depth):
        u = rmsnorm(h, norm1_g[layer])
        proj = matmul([u], [w_in[layer].astype(BF16)], F32)
        o_a = hgrn2(proj, lb_all[layer], hgrn_out_g[layer], batch, length, width)
        qn, kn, vb = qkv_prep(proj, (q_norm_g[layer].reshape(1, HEAD) * qk_scale).astype(F32),
                              k_norm_g[layer].reshape(1, HEAD).astype(F32), width)
        lp = diff_lambda[layer].astype(F32)
        lam_init = 0.8 - 0.6 * math.exp(-0.3 * layer)
        lam = jnp.exp(jnp.sum(lp[0] * lp[1])) - jnp.exp(jnp.sum(lp[2] * lp[3])) + lam_init
        o_b = diff_attention(qn, kn, vb, lam, lam_init, diff_out_g[layer], batch, length)
        w_o = w_out[layer].astype(BF16)
        h = matmul([o_a, o_b], [w_o[:width], w_o[width:]], F32, epilogue="residual", residual=h)
        u = rmsnorm(h, norm2_g[layer])
        z = matmul([u], [w_mlp_up[layer].astype(BF16)], BF16, epilogue="relu2")
        h = matmul_residual_kgrid(z, w_mlp_down[layer].astype(BF16), h)
    return h.reshape(batch, length, d_model)[:, PAD + N_META:]
```

```python
import functools
import math

import numpy as np
import jax
import jax.numpy as jnp
from jax import lax
from jax.experimental import pallas as pl
from jax.experimental.pallas import tpu as pltpu

N_META = 16
ATTN_CHUNK = 64
HEAD = 128
HALF = HEAD // 2
ROW_TILE = 128
PAD = ROW_TILE - N_META
HGRN_CHUNK = 128
HGRN_LEVELS = 7
ONES_ROWS = 16
EPS = 1e-6
MASK_VALUE = -1e30
LOG2E = math.log2(math.e)
V7X_VMEM_LIMIT_BYTES = 56 * 1024 * 1024

F32 = jnp.float32
BF16 = jnp.bfloat16


def _params(semantics):
    return pltpu.CompilerParams(dimension_semantics=semantics, vmem_limit_bytes=V7X_VMEM_LIMIT_BYTES)


def _pick(n, candidates):
    for c in candidates:
        if n % c == 0:
            return c
    raise ValueError(f"no tile for {n} among {candidates}")


def _rmsnorm_kernel(x_ref, g_ref, o_ref):
    x = x_ref[...]
    ms = jnp.mean(x * x, axis=-1, keepdims=True)
    o_ref[...] = (x * lax.rsqrt(ms + EPS) * g_ref[...]).astype(o_ref.dtype)


def rmsnorm(x, g):
    m, d = x.shape
    bm = _pick(m, (384, 256, 128))
    return pl.pallas_call(
        _rmsnorm_kernel,
        grid=(m // bm,),
        in_specs=[pl.BlockSpec((bm, d), lambda i: (i, 0)), pl.BlockSpec((1, d), lambda i: (0, 0))],
        out_specs=pl.BlockSpec((bm, d), lambda i: (i, 0)),
        out_shape=jax.ShapeDtypeStruct((m, d), BF16),
        compiler_params=_params(("parallel",)),
        name="rmsnorm",
    )(x, g.reshape(1, d).astype(F32))


def _matmul_kernel(*refs, n_parts, epilogue):
    xs = refs[:n_parts]
    ws = refs[n_parts:2 * n_parts]
    rest = refs[2 * n_parts:]
    acc = jnp.dot(xs[0][...], ws[0][...].astype(BF16), preferred_element_type=F32)
    for x_ref, w_ref in zip(xs[1:], ws[1:]):
        acc = acc + jnp.dot(x_ref[...], w_ref[...].astype(BF16), preferred_element_type=F32)
    if epilogue == "relu2":
        acc = jnp.square(jnp.maximum(acc, 0.0))
    if epilogue == "residual":
        r_ref, o_ref = rest
        acc = acc + r_ref[...]
    else:
        (o_ref,) = rest
    o_ref[...] = acc.astype(o_ref.dtype)


def matmul(xs, w_stack, layer, out_dtype, epilogue=None, residual=None):
    m = xs[0].shape[0]
    n = w_stack.shape[2]
    bm = _pick(m, (1056, 768, 512, 256, 128))
    bn = _pick(n, (512, 256, 128))
    in_specs = [pl.BlockSpec((bm, x.shape[1]), lambda i, j: (i, 0)) for x in xs]
    row_block = 0
    for x in xs:
        kp = x.shape[1]
        in_specs.append(pl.BlockSpec((None, kp, bn), functools.partial(
            lambda i, j, rb: (layer, rb, j), rb=row_block // kp)))
        row_block += kp
    args = list(xs) + [w_stack] * len(xs)
    if epilogue == "residual":
        in_specs.append(pl.BlockSpec((bm, bn), lambda i, j: (i, j)))
        args.append(residual)
    return pl.pallas_call(
        functools.partial(_matmul_kernel, n_parts=len(xs), epilogue=epilogue),
        grid=(m // bm, n // bn),
        in_specs=in_specs,
        out_specs=pl.BlockSpec((bm, bn), lambda i, j: (i, j)),
        out_shape=jax.ShapeDtypeStruct((m, n), out_dtype),
        compiler_params=_params(("parallel", "parallel")),
        name="matmul_" + (epilogue or "plain"),
    )(*args)


def _matmul_kgrid_kernel(x_ref, w_ref, r_ref, o_ref, acc_ref):
    k = pl.program_id(2)

    @pl.when(k == 0)
    def _():
        acc_ref[...] = r_ref[...]

    acc_ref[...] += jnp.dot(x_ref[...], w_ref[...].astype(BF16), preferred_element_type=F32)

    @pl.when(k == pl.num_programs(2) - 1)
    def _():
        o_ref[...] = acc_ref[...]


def matmul_residual_kgrid(x, w_stack, layer, residual):
    m, kdim = x.shape
    n = w_stack.shape[2]
    bm = _pick(m, (1056, 768, 512, 256, 128))
    bn = _pick(n, (1024, 512, 256, 128))
    bk = _pick(kdim, (2048, 1024, 512, 256, 128))
    return pl.pallas_call(
        _matmul_kgrid_kernel,
        grid=(m // bm, n // bn, kdim // bk),
        in_specs=[
            pl.BlockSpec((bm, bk), lambda i, j, k: (i, k)),
            pl.BlockSpec((None, bk, bn), lambda i, j, k: (layer, k, j)),
            pl.BlockSpec((bm, bn), lambda i, j, k: (i, j)),
        ],
        out_specs=pl.BlockSpec((bm, bn), lambda i, j, k: (i, j)),
        out_shape=jax.ShapeDtypeStruct((m, n), F32),
        scratch_shapes=[pltpu.VMEM((bm, bn), F32)],
        compiler_params=_params(("parallel", "parallel", "arbitrary")),
        name="matmul_residual_kgrid",
    )(x, w_stack, residual)


def _attn_tile(length):
    return _pick(length, (384, 256, 128))


def _qkv_prep_kernel(q_ref, k_ref, v_ref, gq_ref, gk_ref, pool_ref, qo_ref, ko_ref, vt_ref):
    pool = pool_ref[...]
    width = q_ref.shape[2]
    for h in range(width // HEAD):
        cols = slice(h * HEAD, (h + 1) * HEAD)
        for src, g_ref, dst in ((q_ref, gq_ref, qo_ref), (k_ref, gk_ref, ko_ref)):
            x = src[0, :, cols]
            ms = jnp.dot((x * x).astype(BF16), pool, preferred_element_type=F32)
            dst[0, :, cols] = (x * lax.rsqrt(ms + EPS) * g_ref[...]).astype(dst.dtype)
        vt_ref[0, h, 0] = v_ref[0, :, cols].T.astype(vt_ref.dtype)


def qkv_prep(proj3, q_gain, k_gain, width, tile):
    batch, length, _ = proj3.shape
    n_heads = width // HEAD
    nblk = length // tile
    pool = np.kron(np.eye(2), np.full((HALF, HALF), 1.0 / HALF)).astype(np.float32)
    qk = jax.ShapeDtypeStruct((batch, length, width), BF16)
    vt = jax.ShapeDtypeStruct((batch, n_heads, nblk, HEAD, tile), BF16)
    gspec = pl.BlockSpec((1, HEAD), lambda b, i: (0, 0))
    qkspec = pl.BlockSpec((1, tile, width), lambda b, i: (b, i, 0))
    return pl.pallas_call(
        _qkv_prep_kernel,
        grid=(batch, nblk),
        in_specs=[
            pl.BlockSpec((1, tile, width), lambda b, i: (b, i, 4)),
            pl.BlockSpec((1, tile, width), lambda b, i: (b, i, 5)),
            pl.BlockSpec((1, tile, width), lambda b, i: (b, i, 6)),
            gspec, gspec,
            pl.BlockSpec((HEAD, HEAD), lambda b, i: (0, 0)),
        ],
        out_specs=[qkspec, qkspec,
                   pl.BlockSpec((1, n_heads, 1, HEAD, tile), lambda b, i: (b, 0, i, 0, 0))],
        out_shape=[qk, qk, vt],
        compiler_params=_params(("parallel", "parallel")),
        name="qkv_prep",
    )(proj3, proj3, proj3, q_gain, k_gain, jnp.asarray(pool, BF16))


def _attn_constants(n_heads, tile):
    slopes_np = np.exp2(-8.0 * np.arange(1, n_heads + 1, dtype=np.float64) / n_heads) * LOG2E
    ki_np = np.arange(tile)
    feat = (slopes_np[None, :] * ki_np[:, None]).astype(np.float32)
    feat = np.stack([np.where((ki_np < PAD)[:, None], np.float32(MASK_VALUE), feat), feat])
    hi = feat.astype(BF16)
    lo = np.where(feat <= MASK_VALUE, np.float32(0.0), feat - hi.astype(np.float32)).astype(BF16)
    key_feat = np.zeros((2, tile, n_heads, HEAD), BF16)
    key_feat[..., 0] = hi
    key_feat[..., 1] = lo
    key_feat = jnp.asarray(key_feat.reshape(2, tile, n_heads * HEAD))
    slopes = jnp.asarray(slopes_np, F32)
    ki = jnp.arange(tile, dtype=jnp.int32)
    kk = ki[:, None]
    qq = ki[None, :]
    allowed = (kk // ATTN_CHUNK) <= (qq // ATTN_CHUNK)
    rel = (-jnp.abs(qq - kk) + qq - kk).astype(F32)
    diag_bias = jnp.where(allowed[None], slopes[:, None, None] * rel[None], MASK_VALUE)
    lane = jnp.arange(HEAD)
    query_feat = jnp.broadcast_to((lane < 2).astype(BF16)[None, :], (tile, HEAD))
    ones_rows = jnp.broadcast_to((jnp.arange(ONES_ROWS) == 0).astype(BF16)[:, None], (ONES_ROWS, tile))
    return slopes, key_feat, diag_bias, query_feat, ones_rows


def _attn_kernel(scal_ref, slope_ref, q_ref, k_ref, vt_ref, kfeat_ref, qfeat_ref, ones_ref, dbias_ref,
                 g_ref, o_ref, acc_ref, s_ref, *, tile):
    h = pl.program_id(0)
    i = pl.program_id(2)
    lam = scal_ref[0]
    out_scale = scal_ref[1]
    block_shift = slope_ref[h] * float(tile)

    q = q_ref[0]
    lane = lax.broadcasted_iota(jnp.int32, q.shape, 1)
    qfeat = qfeat_ref[...]
    qcat = jnp.concatenate([jnp.concatenate([jnp.where(keep, q, jnp.zeros_like(q)), qfeat], axis=1)
                            for keep in (lane < HALF, lane >= HALF)], axis=0)

    def produce(j, slot, diag=False):
        kb = k_ref[0, pl.ds(pl.multiple_of(j * tile, tile), tile), :]
        kaug = jnp.concatenate([kb, kfeat_ref[jnp.minimum(j, 1)]], axis=1)
        s = lax.dot_general(kaug, qcat, (((1,), (1,)), ((), ())), preferred_element_type=F32)
        if diag:
            s = s + jnp.concatenate([dbias_ref[0]] * 2, axis=1)
        s_ref[slot] = s
        return jnp.max(s, axis=0, keepdims=True)

    def consume(j, slot, col_max, m_old):
        vaug = jnp.concatenate([vt_ref[0, 0, j], ones_ref[...]], axis=0)
        shift = block_shift * (j - i).astype(F32)
        m_new = jnp.maximum(m_old, col_max + shift)
        p = jnp.exp2(s_ref[slot] - (m_new - shift)).astype(BF16)
        acc_ref[...] = acc_ref[...] * jnp.exp2(m_old - m_new) + jnp.dot(vaug, p, preferred_element_type=F32)
        return m_new

    acc_ref[...] = jnp.zeros_like(acc_ref)
    m0 = jnp.full((1, 2 * tile), MASK_VALUE, F32)
    mx0 = produce(i, 0, diag=True)

    def pipe(t, slot, carry):
        m_old, col_max = carry
        m_new = consume(jnp.where(t == 0, i, t - 1), slot, col_max, m_old)
        return m_new, produce(t, 1 - slot)

    carry = lax.fori_loop(0, i // 2, lambda n, c: pipe(2 * n + 1, 1, pipe(2 * n, 0, c)), (m0, mx0))

    def odd_tail(c):
        m_old, col_max = pipe(i - 1, 0, c)
        consume(i - 1, 1, col_max, m_old)
        return 0

    def even_tail(c):
        consume(jnp.where(i == 0, i, i - 1), 0, c[1], c[0])
        return 0

    lax.cond(i % 2 == 1, odd_tail, even_tail, carry)

    def normalized(c):
        cols = slice(c * tile, (c + 1) * tile)
        return acc_ref[:HEAD, cols] / acc_ref[HEAD:HEAD + 1, cols]

    o = normalized(0) - lam * normalized(1)
    o = o * lax.rsqrt(jnp.mean(o * o, axis=0, keepdims=True) + EPS)
    ot = o.T * (g_ref[...] * out_scale)
    row = lax.broadcasted_iota(jnp.int32, ot.shape, 0) + i * tile
    o_ref[0] = jnp.where(row >= PAD, ot, 0.0).astype(o_ref.dtype)


def diff_attention(qn, kn, vt, lam, lam_init, out_g, tile):
    batch, length, width = qn.shape
    n_heads = width // HEAD
    nq = length // tile
    slopes, key_feat, diag_bias, query_feat, ones_rows = _attn_constants(n_heads, tile)
    scal = jnp.stack([lam.astype(F32), jnp.asarray(1.0 - lam_init, F32)])
    smem = pl.BlockSpec(memory_space=pltpu.SMEM)
    return pl.pallas_call(
        functools.partial(_attn_kernel, tile=tile),
        grid=(n_heads, batch, nq),
        in_specs=[
            smem, smem,
            pl.BlockSpec((1, tile, HEAD), lambda h, b, i: (b, i, h)),
            pl.BlockSpec((1, length, HEAD), lambda h, b, i: (b, 0, h)),
            pl.BlockSpec((1, 1, nq, HEAD, tile), lambda h, b, i: (b, h, 0, 0, 0)),
            pl.BlockSpec((2, tile, HEAD), lambda h, b, i: (0, 0, h)),
            pl.BlockSpec((tile, HEAD), lambda h, b, i: (0, 0)),
            pl.BlockSpec((ONES_ROWS, tile), lambda h, b, i: (0, 0)),
            pl.BlockSpec((1, tile, tile), lambda h, b, i: (h, 0, 0)),
            pl.BlockSpec((1, HEAD), lambda h, b, i: (0, 0)),
        ],
        out_specs=pl.BlockSpec((1, tile, HEAD), lambda h, b, i: (b, i, h)),
        out_shape=jax.ShapeDtypeStruct((batch, length, width), BF16),
        scratch_shapes=[pltpu.VMEM((HEAD + ONES_ROWS, 2 * tile), F32),
                        pltpu.VMEM((2, tile, 2 * tile), F32)],
        compiler_params=_params(("parallel", "parallel", "parallel")),
        name="diff_attention",
    )(scal, slopes, qn, kn, vt, key_feat, query_feat, ones_rows, diag_bias,
      out_g.reshape(1, HEAD).astype(F32))


def _hgrn_constants():
    c = HGRN_CHUNK
    t = np.arange(c)[:, None]
    r = np.arange(c)[None, :]
    sums = []
    pair = [np.eye(c)]
    for level in range(1, HGRN_LEVELS + 1):
        size = 2 ** level
        mid = (t // size) * size + size // 2 - 1
        upper = (t % size) >= size // 2
        sums.append(np.where(upper, (r > mid) & (r <= t), (r > t) & (r <= mid)))
        same = (t // size) == (r // size)
        pair.append(same & upper & ((r % size) < size // 2))
    sums.append(r <= t)
    sums.append(r > t)
    sums = np.concatenate(sums, axis=0).astype(np.float32)
    return (jnp.asarray(np.concatenate([sums, sums], axis=1), BF16),
            jnp.asarray(np.stack(pair).astype(np.float32)))


def _hgrn_kernel(q_ref, f_ref, v_ref, g_ref, lb_ref, og_ref, sums_ref, pair_ref, o_ref, state_ref,
                 *, heads, chunks):
    c = HGRN_CHUNK

    @pl.when(pl.program_id(2) == 0)
    def _():
        state_ref[...] = jnp.zeros_like(state_ref)

    def chunk_body(n, _):
        rows = pl.ds(pl.multiple_of(n * c, c), c)
        lb_all = lb_ref[...]
        sig_all = jax.nn.sigmoid(f_ref[0, rows, :])
        logf = jnp.log(lb_all + (1.0 - lb_all) * sig_all)
        k_all = (1.0 - lb_all) * (1.0 - sig_all)
        hi = logf.astype(BF16)
        lo = (logf - hi.astype(F32)).astype(BF16)
        decay_all = jnp.exp(jnp.dot(sums_ref[...], jnp.concatenate([hi, lo], axis=0),
                                    preferred_element_type=F32))
        for h in range(heads):
            cols = slice(h * HEAD, (h + 1) * HEAD)
            q = q_ref[0, rows, cols]
            v = v_ref[0, rows, cols].astype(BF16)
            gate = g_ref[0, rows, cols]
            k = k_all[:, cols]
            decay = decay_all[:, cols]

            a = pair_ref[0] * lax.dot_general(q.astype(BF16), k.astype(BF16), (((1,), (1,)), ((), ())),
                                              preferred_element_type=F32)
            for level in range(1, HGRN_LEVELS + 1):
                d = decay[(level - 1) * c:level * c]
                a = a + pair_ref[level] * lax.dot_general(
                    (q * d).astype(BF16), (k * d).astype(BF16), (((1,), (1,)), ((), ())),
                    preferred_element_type=F32)
            from_start = decay[7 * c:8 * c]
            to_end = decay[8 * c:9 * c]
            state_t = state_ref[h]
            o = jnp.dot(a.astype(BF16), v, preferred_element_type=F32)
            o = o + lax.dot_general((q * from_start).astype(BF16), state_t.astype(BF16),
                                    (((1,), (1,)), ((), ())), preferred_element_type=F32)
            state_ref[h] = state_t * from_start[c - 1:c, :] + lax.dot_general(
                v, (k * to_end).astype(BF16), (((0,), (0,)), ((), ())), preferred_element_type=F32)

            y = o * lax.rsqrt(jnp.mean(o * o, axis=-1, keepdims=True) + EPS) * og_ref[...]
            o_ref[0, rows, cols] = (y * (gate * jax.nn.sigmoid(gate))).astype(o_ref.dtype)
        return 0

    lax.fori_loop(0, chunks, chunk_body, 0)


def hgrn2(proj3, lb, out_g, width):
    batch, length, _ = proj3.shape
    n_heads = width // HEAD
    heads = _pick(n_heads, (4, 2, 1))
    tile = _pick(length, (384, 256, 128))
    groups = n_heads // heads
    bw = heads * HEAD
    sums, pair = _hgrn_constants()

    def seg_spec(seg):
        return pl.BlockSpec((1, tile, bw), lambda b, hg, t: (b, t, seg * groups + hg))

    out = pl.pallas_call(
        functools.partial(_hgrn_kernel, heads=heads, chunks=tile // HGRN_CHUNK),
        grid=(batch, groups, length // tile),
        in_specs=[
            seg_spec(0), seg_spec(1), seg_spec(2), seg_spec(3),
            pl.BlockSpec((1, bw), lambda b, hg, t: (0, hg)),
            pl.BlockSpec((1, HEAD), lambda b, hg, t: (0, 0)),
            pl.BlockSpec(sums.shape, lambda b, hg, t: (0, 0)),
            pl.BlockSpec(pair.shape, lambda b, hg, t: (0, 0, 0)),
        ],
        out_specs=pl.BlockSpec((1, tile, bw), lambda b, hg, t: (b, t, hg)),
        out_shape=jax.ShapeDtypeStruct((batch, length, width), BF16),
        scratch_shapes=[pltpu.VMEM((heads, HEAD, HEAD), F32)],
        compiler_params=_params(("parallel", "parallel", "arbitrary")),
        name="hgrn2",
    )(proj3, proj3, proj3, proj3, lb.reshape(1, width).astype(F32), out_g.reshape(1, HEAD).astype(F32),
      sums, pair)
    return out.reshape(batch * length, width)


def kernel(x, meta_tokens, norm1_g, w_in, hgrn_lb_raw, hgrn_out_g, q_norm_g, k_norm_g, diff_lambda,
           diff_out_g, w_out, norm2_g, w_mlp_up, w_mlp_down):
    batch, seq, d_model = x.shape
    depth = w_in.shape[0]
    width = d_model // 2
    length = PAD + N_META + seq
    tile = _attn_tile(length)
    meta = jnp.broadcast_to(meta_tokens.astype(x.dtype)[None], (batch, N_META, d_model))
    h = jnp.concatenate([jnp.zeros((batch, PAD, d_model), x.dtype), meta, x], axis=1)
    h = h.reshape(batch * length, d_model)

    lb_all = jnp.cumsum(jax.nn.softmax(hgrn_lb_raw.astype(F32), axis=0), axis=0)
    lb_all = lb_all - lb_all[0:1]
    q_scale = LOG2E / math.sqrt(HALF)

    for layer in range(depth):
        u = rmsnorm(h, norm1_g[layer])
        proj3 = matmul([u], w_in, layer, F32).reshape(batch, length, -1)
        o_a = hgrn2(proj3, lb_all[layer], hgrn_out_g[layer], width)
        qn, kn, vt = qkv_prep(proj3, (q_norm_g[layer].reshape(1, HEAD) * q_scale).astype(F32),
                              k_norm_g[layer].reshape(1, HEAD).astype(F32), width, tile)
        lp = diff_lambda[layer].astype(F32)
        lam_init = 0.8 - 0.6 * math.exp(-0.3 * layer)
        lam = jnp.exp(jnp.sum(lp[0] * lp[1])) - jnp.exp(jnp.sum(lp[2] * lp[3])) + lam_init
        o_b = diff_attention(qn, kn, vt, lam, lam_init, diff_out_g[layer], tile)
        h = matmul([o_a, o_b.reshape(batch * length, width)], w_out, layer, F32,
                   epilogue="residual", residual=h)
        u = rmsnorm(h, norm2_g[layer])
        z = matmul([u], w_mlp_up, layer, BF16, epilogue="relu2")
        h = matmul_residual_kgrid(z, w_mlp_down, layer, h)
    return h.reshape(batch, length, d_model)[:, PAD + N_META:]
```

```python
import functools
import math

import numpy as np
import jax
import jax.numpy as jnp
from jax import lax
from jax.experimental import pallas as pl
from jax.experimental.pallas import tpu as pltpu

N_META = 16
ATTN_CHUNK = 64
HEAD = 128
HALF = HEAD // 2
ROW_TILE = 128
PAD = ROW_TILE - N_META
HGRN_CHUNK = 128
HGRN_LEVELS = 7
ONES_ROWS = 16
ATTN_HEADS_PER_STEP = 4
EPS = 1e-6
MASK_VALUE = -1e30
LOG2E = math.log2(math.e)
V7X_VMEM_LIMIT_BYTES = 56 * 1024 * 1024

F32 = jnp.float32
BF16 = jnp.bfloat16


def _params(semantics):
    return pltpu.CompilerParams(dimension_semantics=semantics, vmem_limit_bytes=V7X_VMEM_LIMIT_BYTES)


def _pick(n, candidates):
    for c in candidates:
        if n % c == 0:
            return c
    raise ValueError(f"no tile for {n} among {candidates}")


def _rmsnorm_kernel(x_ref, g_ref, o_ref):
    x = x_ref[...]
    ms = jnp.mean(x * x, axis=-1, keepdims=True)
    o_ref[...] = (x * lax.rsqrt(ms + EPS) * g_ref[...]).astype(o_ref.dtype)


def rmsnorm(x, g):
    m, d = x.shape
    bm = _pick(m, (384, 256, 128))
    return pl.pallas_call(
        _rmsnorm_kernel,
        grid=(m // bm,),
        in_specs=[pl.BlockSpec((bm, d), lambda i: (i, 0)), pl.BlockSpec((1, d), lambda i: (0, 0))],
        out_specs=pl.BlockSpec((bm, d), lambda i: (i, 0)),
        out_shape=jax.ShapeDtypeStruct((m, d), BF16),
        compiler_params=_params(("parallel",)),
        name="rmsnorm",
    )(x, g.reshape(1, d).astype(F32))


def _matmul_kernel(*refs, n_parts, epilogue):
    xs = refs[:n_parts]
    ws = refs[n_parts:2 * n_parts]
    rest = refs[2 * n_parts:]
    acc = jnp.dot(xs[0][...], ws[0][...].astype(BF16), preferred_element_type=F32)
    for x_ref, w_ref in zip(xs[1:], ws[1:]):
        acc = acc + jnp.dot(x_ref[...], w_ref[...].astype(BF16), preferred_element_type=F32)
    if epilogue == "relu2":
        acc = jnp.square(jnp.maximum(acc, 0.0))
    if epilogue == "residual":
        r_ref, o_ref = rest
        acc = acc + r_ref[...]
    else:
        (o_ref,) = rest
    o_ref[...] = acc.astype(o_ref.dtype)


def matmul(xs, w_stack, layer, out_dtype, epilogue=None, residual=None):
    m = xs[0].shape[0]
    n = w_stack.shape[2]
    bm = _pick(m, (1056, 768, 512, 256, 128))
    bn = _pick(n, (512, 256, 128))
    in_specs = [pl.BlockSpec((bm, x.shape[1]), lambda i, j: (i, 0)) for x in xs]
    row_block = 0
    for x in xs:
        kp = x.shape[1]
        in_specs.append(pl.BlockSpec((None, kp, bn), functools.partial(
            lambda i, j, rb: (layer, rb, j), rb=row_block // kp)))
        row_block += kp
    args = list(xs) + [w_stack] * len(xs)
    if epilogue == "residual":
        in_specs.append(pl.BlockSpec((bm, bn), lambda i, j: (i, j)))
        args.append(residual)
    return pl.pallas_call(
        functools.partial(_matmul_kernel, n_parts=len(xs), epilogue=epilogue),
        grid=(m // bm, n // bn),
        in_specs=in_specs,
        out_specs=pl.BlockSpec((bm, bn), lambda i, j: (i, j)),
        out_shape=jax.ShapeDtypeStruct((m, n), out_dtype),
        compiler_params=_params(("parallel", "parallel")),
        name="matmul_" + (epilogue or "plain"),
    )(*args)


def _matmul_kgrid_kernel(x_ref, w_ref, r_ref, o_ref, acc_ref):
    k = pl.program_id(2)

    @pl.when(k == 0)
    def _():
        acc_ref[...] = r_ref[...]

    acc_ref[...] += jnp.dot(x_ref[...], w_ref[...].astype(BF16), preferred_element_type=F32)

    @pl.when(k == pl.num_programs(2) - 1)
    def _():
        o_ref[...] = acc_ref[...]


def matmul_residual_kgrid(x, w_stack, layer, residual):
    m, kdim = x.shape
    n = w_stack.shape[2]
    bm = _pick(m, (1056, 768, 512, 256, 128))
    bn = _pick(n, (1024, 512, 256, 128))
    bk = _pick(kdim, (2048, 1024, 512, 256, 128))
    return pl.pallas_call(
        _matmul_kgrid_kernel,
        grid=(m // bm, n // bn, kdim // bk),
        in_specs=[
            pl.BlockSpec((bm, bk), lambda i, j, k: (i, k)),
            pl.BlockSpec((None, bk, bn), lambda i, j, k: (layer, k, j)),
            pl.BlockSpec((bm, bn), lambda i, j, k: (i, j)),
        ],
        out_specs=pl.BlockSpec((bm, bn), lambda i, j, k: (i, j)),
        out_shape=jax.ShapeDtypeStruct((m, n), F32),
        scratch_shapes=[pltpu.VMEM((bm, bn), F32)],
        compiler_params=_params(("parallel", "parallel", "arbitrary")),
        name="matmul_residual_kgrid",
    )(x, w_stack, residual)


def _attn_tile(length):
    return _pick(length, (384, 256, 128))


def _qkv_prep_kernel(q_ref, k_ref, v_ref, gq_ref, gk_ref, pool_ref, qo_ref, ko_ref, vt_ref):
    pool = pool_ref[...]
    width = q_ref.shape[2]
    for h in range(width // HEAD):
        cols = slice(h * HEAD, (h + 1) * HEAD)
        for src, g_ref, dst in ((q_ref, gq_ref, qo_ref), (k_ref, gk_ref, ko_ref)):
            x = src[0, :, cols]
            ms = jnp.dot((x * x).astype(BF16), pool, preferred_element_type=F32)
            dst[0, :, cols] = (x * lax.rsqrt(ms + EPS) * g_ref[...]).astype(dst.dtype)
        vt_ref[0, h, 0] = v_ref[0, :, cols].T.astype(vt_ref.dtype)


def qkv_prep(proj3, q_gain, k_gain, width, tile):
    batch, length, _ = proj3.shape
    n_heads = width // HEAD
    nblk = length // tile
    pool = np.kron(np.eye(2), np.full((HALF, HALF), 1.0 / HALF)).astype(np.float32)
    qk = jax.ShapeDtypeStruct((batch, length, width), BF16)
    vt = jax.ShapeDtypeStruct((batch, n_heads, nblk, HEAD, tile), BF16)
    gspec = pl.BlockSpec((1, HEAD), lambda b, i: (0, 0))
    qkspec = pl.BlockSpec((1, tile, width), lambda b, i: (b, i, 0))
    return pl.pallas_call(
        _qkv_prep_kernel,
        grid=(batch, nblk),
        in_specs=[
            pl.BlockSpec((1, tile, width), lambda b, i: (b, i, 4)),
            pl.BlockSpec((1, tile, width), lambda b, i: (b, i, 5)),
            pl.BlockSpec((1, tile, width), lambda b, i: (b, i, 6)),
            gspec, gspec,
            pl.BlockSpec((HEAD, HEAD), lambda b, i: (0, 0)),
        ],
        out_specs=[qkspec, qkspec,
                   pl.BlockSpec((1, n_heads, 1, HEAD, tile), lambda b, i: (b, 0, i, 0, 0))],
        out_shape=[qk, qk, vt],
        compiler_params=_params(("parallel", "parallel")),
        name="qkv_prep",
    )(proj3, proj3, proj3, q_gain, k_gain, jnp.asarray(pool, BF16))


def _attn_constants(n_heads, tile):
    slopes_np = np.exp2(-8.0 * np.arange(1, n_heads + 1, dtype=np.float64) / n_heads) * LOG2E
    ki_np = np.arange(tile)
    feat = (slopes_np[None, :] * ki_np[:, None]).astype(np.float32)
    feat = np.stack([np.where((ki_np < PAD)[:, None], np.float32(MASK_VALUE), feat), feat])
    hi = feat.astype(BF16)
    lo = np.where(feat <= MASK_VALUE, np.float32(0.0), feat - hi.astype(np.float32)).astype(BF16)
    key_feat = np.zeros((2, tile, n_heads, HEAD), BF16)
    key_feat[..., 0] = hi
    key_feat[..., 1] = lo
    key_feat = jnp.asarray(key_feat.reshape(2, tile, n_heads * HEAD))
    slopes = jnp.asarray(slopes_np, F32)
    ki = jnp.arange(tile, dtype=jnp.int32)
    kk = ki[:, None]
    qq = ki[None, :]
    allowed = (kk // ATTN_CHUNK) <= (qq // ATTN_CHUNK)
    rel = (-jnp.abs(qq - kk) + qq - kk).astype(F32)
    diag_bias = jnp.where(allowed[None], slopes[:, None, None] * rel[None], MASK_VALUE)
    lane = jnp.arange(HEAD)
    query_feat = jnp.broadcast_to((lane < 2).astype(BF16)[None, :], (tile, HEAD))
    ones_rows = jnp.broadcast_to((jnp.arange(ONES_ROWS) == 0).astype(BF16)[:, None], (ONES_ROWS, tile))
    return slopes, key_feat, diag_bias, query_feat, ones_rows


def _attn_kernel(scal_ref, slope_ref, q_ref, k_ref, vt_ref, kfeat_ref, qfeat_ref, ones_ref, dbias_ref,
                 g_ref, o_ref, acc_ref, s_ref, *, tile, heads):
    i = pl.program_id(2)
    lam = scal_ref[0]
    out_scale = scal_ref[1]
    qfeat = qfeat_ref[...]
    ones = ones_ref[...]
    lane = lax.broadcasted_iota(jnp.int32, (tile, HEAD), 1)

    def head_cols(h):
        return slice(h * HEAD, (h + 1) * HEAD)

    qcat = []
    for h in range(heads):
        q = q_ref[0, :, head_cols(h)]
        qcat.append(jnp.concatenate(
            [jnp.concatenate([jnp.where(keep, q, jnp.zeros_like(q)), qfeat], axis=1)
             for keep in (lane < HALF, lane >= HALF)], axis=0))
    block_shift = [slope_ref[pl.program_id(0) * heads + h] * float(tile) for h in range(heads)]

    def produce(h, j, slot, diag=False):
        kb = k_ref[0, pl.ds(pl.multiple_of(j * tile, tile), tile), head_cols(h)]
        kaug = jnp.concatenate([kb, kfeat_ref[jnp.minimum(j, 1), :, head_cols(h)]], axis=1)
        s = lax.dot_general(kaug, qcat[h], (((1,), (1,)), ((), ())), preferred_element_type=F32)
        if diag:
            s = s + jnp.concatenate([dbias_ref[h]] * 2, axis=1)
        s_ref[h, slot] = s
        return jnp.max(s, axis=0, keepdims=True)

    def consume(h, j, slot, col_max, m_old):
        vaug = jnp.concatenate([vt_ref[0, h, j], ones], axis=0)
        shift = block_shift[h] * (j - i).astype(F32)
        m_new = jnp.maximum(m_old, col_max + shift)
        p = jnp.exp2(s_ref[h, slot] - (m_new - shift)).astype(BF16)
        acc_ref[h] = acc_ref[h] * jnp.exp2(m_old - m_new) + jnp.dot(vaug, p, preferred_element_type=F32)
        return m_new

    acc_ref[...] = jnp.zeros_like(acc_ref)
    m0 = jnp.full((1, 2 * tile), MASK_VALUE, F32)
    init = tuple((m0, produce(h, i, 0, diag=True)) for h in range(heads))

    def pipe(t, slot, carry):
        prev = jnp.where(t == 0, i, t - 1)
        return tuple((consume(h, prev, slot, carry[h][1], carry[h][0]), produce(h, t, 1 - slot))
                     for h in range(heads))

    carry = lax.fori_loop(0, i // 2, lambda n, c: pipe(2 * n + 1, 1, pipe(2 * n, 0, c)), init)

    def odd_tail(c):
        c = pipe(i - 1, 0, c)
        for h in range(heads):
            consume(h, i - 1, 1, c[h][1], c[h][0])
        return 0

    def even_tail(c):
        for h in range(heads):
            consume(h, jnp.where(i == 0, i, i - 1), 0, c[h][1], c[h][0])
        return 0

    lax.cond(i % 2 == 1, odd_tail, even_tail, carry)

    row = lax.broadcasted_iota(jnp.int32, (tile, HEAD), 0) + i * tile
    for h in range(heads):
        def normalized(c):
            cols = slice(c * tile, (c + 1) * tile)
            return acc_ref[h, :HEAD, cols] / acc_ref[h, HEAD:HEAD + 1, cols]

        o = normalized(0) - lam * normalized(1)
        o = o * lax.rsqrt(jnp.mean(o * o, axis=0, keepdims=True) + EPS)
        ot = o.T * (g_ref[...] * out_scale)
        o_ref[0, :, head_cols(h)] = jnp.where(row >= PAD, ot, 0.0).astype(o_ref.dtype)


def diff_attention(qn, kn, vt, lam, lam_init, out_g, tile):
    batch, length, width = qn.shape
    n_heads = width // HEAD
    heads = _pick(n_heads, (ATTN_HEADS_PER_STEP, 1))
    nq = length // tile
    slopes, key_feat, diag_bias, query_feat, ones_rows = _attn_constants(n_heads, tile)
    scal = jnp.stack([lam.astype(F32), jnp.asarray(1.0 - lam_init, F32)])
    smem = pl.BlockSpec(memory_space=pltpu.SMEM)
    bw = heads * HEAD
    return pl.pallas_call(
        functools.partial(_attn_kernel, tile=tile, heads=heads),
        grid=(n_heads // heads, batch, nq),
        in_specs=[
            smem, smem,
            pl.BlockSpec((1, tile, bw), lambda g, b, i: (b, i, g)),
            pl.BlockSpec((1, length, bw), lambda g, b, i: (b, 0, g)),
            pl.BlockSpec((1, heads, nq, HEAD, tile), lambda g, b, i: (b, g, 0, 0, 0)),
            pl.BlockSpec((2, tile, bw), lambda g, b, i: (0, 0, g)),
            pl.BlockSpec((tile, HEAD), lambda g, b, i: (0, 0)),
            pl.BlockSpec((ONES_ROWS, tile), lambda g, b, i: (0, 0)),
            pl.BlockSpec((heads, tile, tile), lambda g, b, i: (g, 0, 0)),
            pl.BlockSpec((1, HEAD), lambda g, b, i: (0, 0)),
        ],
        out_specs=pl.BlockSpec((1, tile, bw), lambda g, b, i: (b, i, g)),
        out_shape=jax.ShapeDtypeStruct((batch, length, width), BF16),
        scratch_shapes=[pltpu.VMEM((heads, HEAD + ONES_ROWS, 2 * tile), F32),
                        pltpu.VMEM((heads, 2, tile, 2 * tile), F32)],
        compiler_params=_params(("parallel", "parallel", "parallel")),
        name="diff_attention",
    )(scal, slopes, qn, kn, vt, key_feat, query_feat, ones_rows, diag_bias,
      out_g.reshape(1, HEAD).astype(F32))


def _hgrn_constants():
    c = HGRN_CHUNK
    t = np.arange(c)[:, None]
    r = np.arange(c)[None, :]
    sums = []
    pair = [np.eye(c)]
    for level in range(1, HGRN_LEVELS + 1):
        size = 2 ** level
        mid = (t // size) * size + size // 2 - 1
        upper = (t % size) >= size // 2
        sums.append(np.where(upper, (r > mid) & (r <= t), (r > t) & (r <= mid)))
        same = (t // size) == (r // size)
        pair.append(same & upper & ((r % size) < size // 2))
    sums.append(r <= t)
    sums.append(r > t)
    sums = np.concatenate(sums, axis=0).astype(np.float32)
    return (jnp.asarray(np.concatenate([sums, sums], axis=1), BF16),
            jnp.asarray(np.stack(pair).astype(np.float32)))


def _hgrn_kernel(q_ref, f_ref, v_ref, g_ref, lb_ref, og_ref, sums_ref, pair_ref, o_ref, state_ref,
                 *, heads, chunks):
    c = HGRN_CHUNK

    @pl.when(pl.program_id(2) == 0)
    def _():
        state_ref[...] = jnp.zeros_like(state_ref)

    def chunk_body(n, _):
        rows = pl.ds(pl.multiple_of(n * c, c), c)
        lb_all = lb_ref[...]
        sig_all = jax.nn.sigmoid(f_ref[0, rows, :])
        logf = jnp.log(lb_all + (1.0 - lb_all) * sig_all)
        k_all = (1.0 - lb_all) * (1.0 - sig_all)
        hi = logf.astype(BF16)
        lo = (logf - hi.astype(F32)).astype(BF16)
        decay_all = jnp.exp(jnp.dot(sums_ref[...], jnp.concatenate([hi, lo], axis=0),
                                    preferred_element_type=F32))
        for h in range(heads):
            cols = slice(h * HEAD, (h + 1) * HEAD)
            q = q_ref[0, rows, cols]
            v = v_ref[0, rows, cols].astype(BF16)
            gate = g_ref[0, rows, cols]
            k = k_all[:, cols]
            decay = decay_all[:, cols]

            a = pair_ref[0] * lax.dot_general(q.astype(BF16), k.astype(BF16), (((1,), (1,)), ((), ())),
                                              preferred_element_type=F32)
            for level in range(1, HGRN_LEVELS + 1):
                d = decay[(level - 1) * c:level * c]
                a = a + pair_ref[level] * lax.dot_general(
                    (q * d).astype(BF16), (k * d).astype(BF16), (((1,), (1,)), ((), ())),
                    preferred_element_type=F32)
            from_start = decay[7 * c:8 * c]
            to_end = decay[8 * c:9 * c]
            state_t = state_ref[h]
            o = jnp.dot(a.astype(BF16), v, preferred_element_type=F32)
            o = o + lax.dot_general((q * from_start).astype(BF16), state_t.astype(BF16),
                                    (((1,), (1,)), ((), ())), preferred_element_type=F32)
            state_ref[h] = state_t * from_start[c - 1:c, :] + lax.dot_general(
                v, (k * to_end).astype(BF16), (((0,), (0,)), ((), ())), preferred_element_type=F32)

            y = o * lax.rsqrt(jnp.mean(o * o, axis=-1, keepdims=True) + EPS) * og_ref[...]
            o_ref[0, rows, cols] = (y * (gate * jax.nn.sigmoid(gate))).astype(o_ref.dtype)
        return 0

    lax.fori_loop(0, chunks, chunk_body, 0)


def hgrn2(proj3, lb, out_g, width):
    batch, length, _ = proj3.shape
    n_heads = width // HEAD
    heads = _pick(n_heads, (4, 2, 1))
    tile = _pick(length, (384, 256, 128))
    groups = n_heads // heads
    bw = heads * HEAD
    sums, pair = _hgrn_constants()

    def seg_spec(seg):
        return pl.BlockSpec((1, tile, bw), lambda b, hg, t: (b, t, seg * groups + hg))

    out = pl.pallas_call(
        functools.partial(_hgrn_kernel, heads=heads, chunks=tile // HGRN_CHUNK),
        grid=(batch, groups, length // tile),
        in_specs=[
            seg_spec(0), seg_spec(1), seg_spec(2), seg_spec(3),
            pl.BlockSpec((1, bw), lambda b, hg, t: (0, hg)),
            pl.BlockSpec((1, HEAD), lambda b, hg, t: (0, 0)),
            pl.BlockSpec(sums.shape, lambda b, hg, t: (0, 0)),
            pl.BlockSpec(pair.shape, lambda b, hg, t: (0, 0, 0)),
        ],
        out_specs=pl.BlockSpec((1, tile, bw), lambda b, hg, t: (b, t, hg)),
        out_shape=jax.ShapeDtypeStruct((batch, length, width), BF16),
        scratch_shapes=[pltpu.VMEM((heads, HEAD, HEAD), F32)],
        compiler_params=_params(("parallel", "parallel", "arbitrary")),
        name="hgrn2",
    )(proj3, proj3, proj3, proj3, lb.reshape(1, width).astype(F32), out_g.reshape(1, HEAD).astype(F32),
      sums, pair)
    return out.reshape(batch * length, width)


def kernel(x, meta_tokens, norm1_g, w_in, hgrn_lb_raw, hgrn_out_g, q_norm_g, k_norm_g, diff_lambda,
           diff_out_g, w_out, norm2_g, w_mlp_up, w_mlp_down):
    batch, seq, d_model = x.shape
    depth = w_in.shape[0]
    width = d_model // 2
    length = PAD + N_META + seq
    tile = _attn_tile(length)
    meta = jnp.broadcast_to(meta_tokens.astype(x.dtype)[None], (batch, N_META, d_model))
    h = jnp.concatenate([jnp.zeros((batch, PAD, d_model), x.dtype), meta, x], axis=1)
    h = h.reshape(batch * length, d_model)

    lb_all = jnp.cumsum(jax.nn.softmax(hgrn_lb_raw.astype(F32), axis=0), axis=0)
    lb_all = lb_all - lb_all[0:1]
    q_scale = LOG2E / math.sqrt(HALF)

    for layer in range(depth):
        u = rmsnorm(h, norm1_g[layer])
        proj3 = matmul([u], w_in, layer, F32).reshape(batch, length, -1)
        o_a = hgrn2(proj3, lb_all[layer], hgrn_out_g[layer], width)
        qn, kn, vt = qkv_prep(proj3, (q_norm_g[layer].reshape(1, HEAD) * q_scale).astype(F32),
                              k_norm_g[layer].reshape(1, HEAD).astype(F32), width, tile)
        lp = diff_lambda[layer].astype(F32)
        lam_init = 0.8 - 0.6 * math.exp(-0.3 * layer)
        lam = jnp.exp(jnp.sum(lp[0] * lp[1])) - jnp.exp(jnp.sum(lp[2] * lp[3])) + lam_init
        o_b = diff_attention(qn, kn, vt, lam, lam_init, diff_out_g[layer], tile)
        h = matmul([o_a, o_b.reshape(batch * length, width)], w_out, layer, F32,
                   epilogue="residual", residual=h)
        u = rmsnorm(h, norm2_g[layer])
        z = matmul([u], w_mlp_up, layer, BF16, epilogue="relu2")
        h = matmul_residual_kgrid(z, w_mlp_down, layer, h)
    return h.reshape(batch, length, d_model)[:, PAD + N_META:]
```

```python
import functools
import math

import numpy as np
import jax
import jax.numpy as jnp
from jax import lax
from jax.experimental import pallas as pl
from jax.experimental.pallas import tpu as pltpu

N_META = 16
ATTN_CHUNK = 64
HEAD = 128
LANES = 128
HALF = HEAD // 2
ROW_TILE = 128
PAD = ROW_TILE - N_META
HGRN_CHUNK = 128
HGRN_LEVELS = 7
ONES_ROWS = 16
ATTN_HEADS_PER_STEP = 4
EPS = 1e-6
MASK_VALUE = -1e30
LOG2E = math.log2(math.e)
V7X_VMEM_LIMIT_BYTES = 56 * 1024 * 1024
DENSE_WINDOW_BUDGET_BYTES = 50 * 1024 * 1024

F32 = jnp.float32
BF16 = jnp.bfloat16


def _params(semantics):
    return pltpu.CompilerParams(dimension_semantics=semantics, vmem_limit_bytes=V7X_VMEM_LIMIT_BYTES)


def _pick(n, candidates):
    for c in candidates:
        if n % c == 0:
            return c
    raise ValueError(f"no tile for {n} among {candidates}")


def _rmsnorm_kernel(x_ref, g_ref, o_ref):
    x = x_ref[...]
    ms = jnp.mean(x * x, axis=-1, keepdims=True)
    o_ref[...] = (x * lax.rsqrt(ms + EPS) * g_ref[...]).astype(o_ref.dtype)


def rmsnorm(x, g):
    m, d = x.shape
    bm = _pick(m, (384, 256, 128))
    return pl.pallas_call(
        _rmsnorm_kernel,
        grid=(m // bm,),
        in_specs=[pl.BlockSpec((bm, d), lambda i: (i, 0)), pl.BlockSpec((1, d), lambda i: (0, 0))],
        out_specs=pl.BlockSpec((bm, d), lambda i: (i, 0)),
        out_shape=jax.ShapeDtypeStruct((m, d), BF16),
        compiler_params=_params(("parallel",)),
        name="rmsnorm",
    )(x, g.reshape(1, d).astype(F32))


def _row_scale(ssq_ref, d_model, n_cols):
    r = lax.rsqrt(ssq_ref[...] * (1.0 / d_model) + EPS)
    return jnp.concatenate([r] * (n_cols // LANES), axis=1)


def _emit_normed(h, col_block, gain_ref, hb_ref, ssq_ref):
    hb_ref[...] = (h * gain_ref[...]).astype(hb_ref.dtype)
    part = jnp.broadcast_to(jnp.sum(h * h, axis=1, keepdims=True), ssq_ref.shape)

    @pl.when(col_block == 0)
    def _():
        ssq_ref[...] = part

    @pl.when(col_block != 0)
    def _():
        ssq_ref[...] += part


def _matmul_kernel(*refs, n_parts, epilogue, d_model, scaled, emit):
    xs = refs[:n_parts]
    ws = refs[n_parts:2 * n_parts]
    rest = list(refs[2 * n_parts:])
    acc = jnp.dot(xs[0][...], ws[0][...].astype(BF16), preferred_element_type=F32)
    for x_ref, w_ref in zip(xs[1:], ws[1:]):
        acc = acc + jnp.dot(x_ref[...], w_ref[...].astype(BF16), preferred_element_type=F32)
    if scaled:
        acc = acc * _row_scale(rest.pop(0), d_model, acc.shape[1])
    if epilogue == "relu2":
        acc = jnp.square(jnp.maximum(acc, 0.0))
    if epilogue == "residual":
        acc = acc + rest.pop(0)[...]
    if emit:
        gain_ref = rest.pop(0)
        o_ref, hb_ref, ssq_ref = rest
        _emit_normed(acc, pl.program_id(1), gain_ref, hb_ref, ssq_ref)
    else:
        (o_ref,) = rest
    o_ref[...] = acc.astype(o_ref.dtype)


def matmul(xs, w_stack, layer, out_dtype, epilogue=None, residual=None, row_ssq=None, next_gain=None):
    m = xs[0].shape[0]
    n = w_stack.shape[2]
    d_model = sum(x.shape[1] for x in xs)
    bn = _pick(n, (512, 256, 128))

    def vmem_bytes(bm):
        per_step = bm * d_model * 2 + d_model * bn * 4 + bm * bn * jnp.dtype(out_dtype).itemsize
        if row_ssq is not None:
            per_step += bm * LANES * 4
        if epilogue == "residual":
            per_step += bm * bn * 4
        if next_gain is not None:
            per_step += bm * bn * 2 + bm * LANES * 4
        return 2 * per_step

    bm = next(c for c in (1408, 1056, 768, 512, 256, 128)
              if m % c == 0 and vmem_bytes(c) <= DENSE_WINDOW_BUDGET_BYTES)
    in_specs = [pl.BlockSpec((bm, x.shape[1]), lambda i, j: (i, 0)) for x in xs]
    row_block = 0
    for x in xs:
        kp = x.shape[1]
        in_specs.append(pl.BlockSpec((None, kp, bn), functools.partial(
            lambda i, j, rb: (layer, rb, j), rb=row_block // kp)))
        row_block += kp
    args = list(xs) + [w_stack] * len(xs)
    if row_ssq is not None:
        in_specs.append(pl.BlockSpec((bm, LANES), lambda i, j: (i, 0)))
        args.append(row_ssq)
    if epilogue == "residual":
        in_specs.append(pl.BlockSpec((bm, bn), lambda i, j: (i, j)))
        args.append(residual)
    tile = pl.BlockSpec((bm, bn), lambda i, j: (i, j))
    out_specs, out_shape = tile, jax.ShapeDtypeStruct((m, n), out_dtype)
    if next_gain is not None:
        in_specs.append(pl.BlockSpec((1, bn), lambda i, j: (0, j)))
        args.append(next_gain.reshape(1, n).astype(F32))
        out_specs = [tile, tile, pl.BlockSpec((bm, LANES), lambda i, j: (i, 0))]
        out_shape = [out_shape, jax.ShapeDtypeStruct((m, n), BF16), jax.ShapeDtypeStruct((m, LANES), F32)]
    return pl.pallas_call(
        functools.partial(_matmul_kernel, n_parts=len(xs), epilogue=epilogue, d_model=d_model,
                          scaled=row_ssq is not None, emit=next_gain is not None),
        grid=(m // bm, n // bn),
        in_specs=in_specs,
        out_specs=out_specs,
        out_shape=out_shape,
        compiler_params=_params(("parallel", "arbitrary" if next_gain is not None else "parallel")),
        name="matmul_" + (epilogue or "plain"),
    )(*args)


def _matmul_kgrid_kernel(x_ref, w_ref, r_ref, o_ref, acc_ref):
    k = pl.program_id(2)

    @pl.when(k == 0)
    def _():
        acc_ref[...] = r_ref[...]

    acc_ref[...] += jnp.dot(x_ref[...], w_ref[...].astype(BF16), preferred_element_type=F32)

    @pl.when(k == pl.num_programs(2) - 1)
    def _():
        o_ref[...] = acc_ref[...]


def matmul_residual_kgrid(x, w_stack, layer, residual):
    m, kdim = x.shape
    n = w_stack.shape[2]
    bm = _pick(m, (1056, 768, 512, 256, 128))
    bn = _pick(n, (1024, 512, 256, 128))
    bk = _pick(kdim, (2048, 1024, 512, 256, 128))
    return pl.pallas_call(
        _matmul_kgrid_kernel,
        grid=(m // bm, n // bn, kdim // bk),
        in_specs=[
            pl.BlockSpec((bm, bk), lambda i, j, k: (i, k)),
            pl.BlockSpec((None, bk, bn), lambda i, j, k: (layer, k, j)),
            pl.BlockSpec((bm, bn), lambda i, j, k: (i, j)),
        ],
        out_specs=pl.BlockSpec((bm, bn), lambda i, j, k: (i, j)),
        out_shape=jax.ShapeDtypeStruct((m, n), F32),
        scratch_shapes=[pltpu.VMEM((bm, bn), F32)],
        compiler_params=_params(("parallel", "parallel", "arbitrary")),
        name="matmul_residual_kgrid",
    )(x, w_stack, residual)


def _attn_tile(length):
    return _pick(length, (384, 256, 128))


def _qkv_prep_kernel(q_ref, k_ref, v_ref, gq_ref, gk_ref, pool_ref, qo_ref, ko_ref, vt_ref):
    pool = pool_ref[...]
    width = q_ref.shape[2]
    for h in range(width // HEAD):
        cols = slice(h * HEAD, (h + 1) * HEAD)
        for src, g_ref, dst in ((q_ref, gq_ref, qo_ref), (k_ref, gk_ref, ko_ref)):
            x = src[0, :, cols]
            ms = jnp.dot((x * x).astype(BF16), pool, preferred_element_type=F32)
            dst[0, :, cols] = (x * lax.rsqrt(ms + EPS) * g_ref[...]).astype(dst.dtype)
        vt_ref[0, h, 0] = v_ref[0, :, cols].T.astype(vt_ref.dtype)


def qkv_prep(proj3, q_gain, k_gain, width, tile):
    batch, length, _ = proj3.shape
    n_heads = width // HEAD
    nblk = length // tile
    pool = np.kron(np.eye(2), np.full((HALF, HALF), 1.0 / HALF)).astype(np.float32)
    qk = jax.ShapeDtypeStruct((batch, length, width), BF16)
    vt = jax.ShapeDtypeStruct((batch, n_heads, nblk, HEAD, tile), BF16)
    gspec = pl.BlockSpec((1, HEAD), lambda b, i: (0, 0))
    qkspec = pl.BlockSpec((1, tile, width), lambda b, i: (b, i, 0))
    return pl.pallas_call(
        _qkv_prep_kernel,
        grid=(batch, nblk),
        in_specs=[
            pl.BlockSpec((1, tile, width), lambda b, i: (b, i, 4)),
            pl.BlockSpec((1, tile, width), lambda b, i: (b, i, 5)),
            pl.BlockSpec((1, tile, width), lambda b, i: (b, i, 6)),
            gspec, gspec,
            pl.BlockSpec((HEAD, HEAD), lambda b, i: (0, 0)),
        ],
        out_specs=[qkspec, qkspec,
                   pl.BlockSpec((1, n_heads, 1, HEAD, tile), lambda b, i: (b, 0, i, 0, 0))],
        out_shape=[qk, qk, vt],
        compiler_params=_params(("parallel", "parallel")),
        name="qkv_prep",
    )(proj3, proj3, proj3, q_gain, k_gain, jnp.asarray(pool, BF16))


def _attn_constants(n_heads, tile):
    slopes_np = np.exp2(-8.0 * np.arange(1, n_heads + 1, dtype=np.float64) / n_heads) * LOG2E
    ki_np = np.arange(tile)
    feat = (slopes_np[None, :] * ki_np[:, None]).astype(np.float32)
    feat = np.stack([np.where((ki_np < PAD)[:, None], np.float32(MASK_VALUE), feat), feat])
    hi = feat.astype(BF16)
    lo = np.where(feat <= MASK_VALUE, np.float32(0.0), feat - hi.astype(np.float32)).astype(BF16)
    key_feat = np.zeros((2, tile, n_heads, HEAD), BF16)
    key_feat[..., 0] = hi
    key_feat[..., 1] = lo
    key_feat = jnp.asarray(key_feat.reshape(2, tile, n_heads * HEAD))
    slopes = jnp.asarray(slopes_np, F32)
    ki = jnp.arange(tile, dtype=jnp.int32)
    kk = ki[:, None]
    qq = ki[None, :]
    allowed = (kk // ATTN_CHUNK) <= (qq // ATTN_CHUNK)
    rel = (-jnp.abs(qq - kk) + qq - kk).astype(F32)
    diag_bias = jnp.where(allowed[None], slopes[:, None, None] * rel[None], MASK_VALUE)
    lane = jnp.arange(HEAD)
    query_feat = jnp.broadcast_to((lane < 2).astype(BF16)[None, :], (tile, HEAD))
    ones_rows = jnp.broadcast_to((jnp.arange(ONES_ROWS) == 0).astype(BF16)[:, None], (ONES_ROWS, tile))
    return slopes, key_feat, diag_bias, query_feat, ones_rows


def _attn_kernel(scal_ref, slope_ref, q_ref, k_ref, vt_ref, kfeat_ref, qfeat_ref, ones_ref, dbias_ref,
                 g_ref, o_ref, acc_ref, s_ref, *, tile, heads):
    i = pl.program_id(2)
    lam = scal_ref[0]
    out_scale = scal_ref[1]
    qfeat = qfeat_ref[...]
    ones = ones_ref[...]
    lane = lax.broadcasted_iota(jnp.int32, (tile, HEAD), 1)

    def head_cols(h):
        return slice(h * HEAD, (h + 1) * HEAD)

    qcat = []
    for h in range(heads):
        q = q_ref[0, :, head_cols(h)]
        qcat.append(jnp.concatenate(
            [jnp.concatenate([jnp.where(keep, q, jnp.zeros_like(q)), qfeat], axis=1)
             for keep in (lane < HALF, lane >= HALF)], axis=0))
    block_shift = [slope_ref[pl.program_id(0) * heads + h] * float(tile) for h in range(heads)]

    def produce(h, j, slot, diag=False):
        kb = k_ref[0, pl.ds(pl.multiple_of(j * tile, tile), tile), head_cols(h)]
        kaug = jnp.concatenate([kb, kfeat_ref[jnp.minimum(j, 1), :, head_cols(h)]], axis=1)
        s = lax.dot_general(kaug, qcat[h], (((1,), (1,)), ((), ())), preferred_element_type=F32)
        if diag:
            s = s + jnp.concatenate([dbias_ref[h]] * 2, axis=1)
        s_ref[h, slot] = s
        return jnp.max(s, axis=0, keepdims=True)

    def consume(h, j, slot, col_max, m_old):
        vaug = jnp.concatenate([vt_ref[0, h, j], ones], axis=0)
        shift = block_shift[h] * (j - i).astype(F32)
        m_new = jnp.maximum(m_old, col_max + shift)
        p = jnp.exp2(s_ref[h, slot] - (m_new - shift)).astype(BF16)
        acc_ref[h] = acc_ref[h] * jnp.exp2(m_old - m_new) + jnp.dot(vaug, p, preferred_element_type=F32)
        return m_new

    acc_ref[...] = jnp.zeros_like(acc_ref)
    m0 = jnp.full((1, 2 * tile), MASK_VALUE, F32)
    init = tuple((m0, produce(h, i, 0, diag=True)) for h in range(heads))

    def pipe(t, slot, carry):
        prev = jnp.where(t == 0, i, t - 1)
        return tuple((consume(h, prev, slot, carry[h][1], carry[h][0]), produce(h, t, 1 - slot))
                     for h in range(heads))

    carry = lax.fori_loop(0, i // 2, lambda n, c: pipe(2 * n + 1, 1, pipe(2 * n, 0, c)), init)

    def odd_tail(c):
        c = pipe(i - 1, 0, c)
        for h in range(heads):
            consume(h, i - 1, 1, c[h][1], c[h][0])
        return 0

    def even_tail(c):
        for h in range(heads):
            consume(h, jnp.where(i == 0, i, i - 1), 0, c[h][1], c[h][0])
        return 0

    lax.cond(i % 2 == 1, odd_tail, even_tail, carry)

    row = lax.broadcasted_iota(jnp.int32, (tile, HEAD), 0) + i * tile
    for h in range(heads):
        def normalized(c):
            cols = slice(c * tile, (c + 1) * tile)
            return acc_ref[h, :HEAD, cols] / acc_ref[h, HEAD:HEAD + 1, cols]

        o = normalized(0) - lam * normalized(1)
        o = o * lax.rsqrt(jnp.mean(o * o, axis=0, keepdims=True) + EPS)
        ot = o.T * (g_ref[...] * out_scale)
        o_ref[0, :, head_cols(h)] = jnp.where(row >= PAD, ot, 0.0).astype(o_ref.dtype)


def diff_attention(qn, kn, vt, lam, lam_init, out_g, tile):
    batch, length, width = qn.shape
    n_heads = width // HEAD
    heads = _pick(n_heads, (ATTN_HEADS_PER_STEP, 1))
    nq = length // tile
    slopes, key_feat, diag_bias, query_feat, ones_rows = _attn_constants(n_heads, tile)
    scal = jnp.stack([lam.astype(F32), jnp.asarray(1.0 - lam_init, F32)])
    smem = pl.BlockSpec(memory_space=pltpu.SMEM)
    bw = heads * HEAD
    return pl.pallas_call(
        functools.partial(_attn_kernel, tile=tile, heads=heads),
        grid=(n_heads // heads, batch, nq),
        in_specs=[
            smem, smem,
            pl.BlockSpec((1, tile, bw), lambda g, b, i: (b, i, g)),
            pl.BlockSpec((1, length, bw), lambda g, b, i: (b, 0, g)),
            pl.BlockSpec((1, heads, nq, HEAD, tile), lambda g, b, i: (b, g, 0, 0, 0)),
            pl.BlockSpec((2, tile, bw), lambda g, b, i: (0, 0, g)),
            pl.BlockSpec((tile, HEAD), lambda g, b, i: (0, 0)),
            pl.BlockSpec((ONES_ROWS, tile), lambda g, b, i: (0, 0)),
            pl.BlockSpec((heads, tile, tile), lambda g, b, i: (g, 0, 0)),
            pl.BlockSpec((1, HEAD), lambda g, b, i: (0, 0)),
        ],
        out_specs=pl.BlockSpec((1, tile, bw), lambda g, b, i: (b, i, g)),
        out_shape=jax.ShapeDtypeStruct((batch, length, width), BF16),
        scratch_shapes=[pltpu.VMEM((heads, HEAD + ONES_ROWS, 2 * tile), F32),
                        pltpu.VMEM((heads, 2, tile, 2 * tile), F32)],
        compiler_params=_params(("parallel", "parallel", "parallel")),
        name="diff_attention",
    )(scal, slopes, qn, kn, vt, key_feat, query_feat, ones_rows, diag_bias,
      out_g.reshape(1, HEAD).astype(F32))


def _hgrn_constants():
    c = HGRN_CHUNK
    t = np.arange(c)[:, None]
    r = np.arange(c)[None, :]
    sums = []
    pair = [np.eye(c)]
    for level in range(1, HGRN_LEVELS + 1):
        size = 2 ** level
        mid = (t // size) * size + size // 2 - 1
        upper = (t % size) >= size // 2
        sums.append(np.where(upper, (r > mid) & (r <= t), (r > t) & (r <= mid)))
        same = (t // size) == (r // size)
        pair.append(same & upper & ((r % size) < size // 2))
    sums.append(r <= t)
    sums.append(r > t)
    sums = np.concatenate(sums, axis=0).astype(np.float32)
    return (jnp.asarray(np.concatenate([sums, sums], axis=1), BF16),
            jnp.asarray(np.stack(pair).astype(np.float32)))


def _hgrn_kernel(q_ref, f_ref, v_ref, g_ref, lb_ref, og_ref, sums_ref, pair_ref, o_ref, state_ref,
                 *, heads, chunks):
    c = HGRN_CHUNK

    @pl.when(pl.program_id(2) == 0)
    def _():
        state_ref[...] = jnp.zeros_like(state_ref)

    def chunk_body(n, _):
        rows = slice(n * c, (n + 1) * c)
        lb_all = lb_ref[...]
        sig_all = jax.nn.sigmoid(f_ref[0, rows, :])
        logf = jnp.log2(lb_all + (1.0 - lb_all) * sig_all)
        k_all = ((1.0 - lb_all) * (1.0 - sig_all)).astype(BF16)
        hi = logf.astype(BF16)
        lo = (logf - hi.astype(F32)).astype(BF16)
        decay_all = jnp.exp2(jnp.dot(sums_ref[...], jnp.concatenate([hi, lo], axis=0),
                                     preferred_element_type=F32))
        for h in range(heads):
            cols = slice(h * HEAD, (h + 1) * HEAD)
            q = q_ref[0, rows, cols].astype(BF16)
            v = v_ref[0, rows, cols].astype(BF16)
            gate = g_ref[0, rows, cols]
            k = k_all[:, cols]
            decay = decay_all[:, cols]

            a = pair_ref[0] * lax.dot_general(q, k, (((1,), (1,)), ((), ())), preferred_element_type=F32)
            for level in range(1, HGRN_LEVELS + 1):
                d = decay[(level - 1) * c:level * c].astype(BF16)
                a = a + pair_ref[level] * lax.dot_general(q * d, k * d, (((1,), (1,)), ((), ())),
                                                          preferred_element_type=F32)
            from_start = decay[7 * c:8 * c]
            to_end = decay[8 * c:9 * c].astype(BF16)
            state_t = state_ref[h]
            o = jnp.dot(a.astype(BF16), v, preferred_element_type=F32)
            o = o + lax.dot_general(q * from_start.astype(BF16), state_t.astype(BF16),
                                    (((1,), (1,)), ((), ())), preferred_element_type=F32)
            state_ref[h] = state_t * from_start[c - 1:c, :] + lax.dot_general(
                v, k * to_end, (((0,), (0,)), ((), ())), preferred_element_type=F32)

            y = o * lax.rsqrt(jnp.mean(o * o, axis=-1, keepdims=True) + EPS) * og_ref[...]
            o_ref[0, rows, cols] = (y * (gate * jax.nn.sigmoid(gate))).astype(o_ref.dtype)
        return 0

    for n in range(chunks):
        chunk_body(n, 0)


def hgrn2(proj3, lb, out_g, width):
    batch, length, _ = proj3.shape
    n_heads = width // HEAD
    heads = _pick(n_heads, (4, 2, 1))
    tile = _pick(length, (384, 256, 128))
    groups = n_heads // heads
    bw = heads * HEAD
    sums, pair = _hgrn_constants()

    def seg_spec(seg):
        return pl.BlockSpec((1, tile, bw), lambda b, hg, t: (b, t, seg * groups + hg))

    out = pl.pallas_call(
        functools.partial(_hgrn_kernel, heads=heads, chunks=tile // HGRN_CHUNK),
        grid=(batch, groups, length // tile),
        in_specs=[
            seg_spec(0), seg_spec(1), seg_spec(2), seg_spec(3),
            pl.BlockSpec((1, bw), lambda b, hg, t: (0, hg)),
            pl.BlockSpec((1, HEAD), lambda b, hg, t: (0, 0)),
            pl.BlockSpec(sums.shape, lambda b, hg, t: (0, 0)),
            pl.BlockSpec(pair.shape, lambda b, hg, t: (0, 0, 0)),
        ],
        out_specs=pl.BlockSpec((1, tile, bw), lambda b, hg, t: (b, t, hg)),
        out_shape=jax.ShapeDtypeStruct((batch, length, width), BF16),
        scratch_shapes=[pltpu.VMEM((heads, HEAD, HEAD), F32)],
        compiler_params=_params(("parallel", "parallel", "arbitrary")),
        name="hgrn2",
    )(proj3, proj3, proj3, proj3, lb.reshape(1, width).astype(F32), out_g.reshape(1, HEAD).astype(F32),
      sums, pair)
    return out.reshape(batch * length, width)


def kernel(x, meta_tokens, norm1_g, w_in, hgrn_lb_raw, hgrn_out_g, q_norm_g, k_norm_g, diff_lambda,
           diff_out_g, w_out, norm2_g, w_mlp_up, w_mlp_down):
    batch, seq, d_model = x.shape
    depth = w_in.shape[0]
    width = d_model // 2
    length = PAD + N_META + seq
    tile = _attn_tile(length)
    meta = jnp.broadcast_to(meta_tokens.astype(x.dtype)[None], (batch, N_META, d_model))
    h = jnp.concatenate([jnp.zeros((batch, PAD, d_model), x.dtype), meta, x], axis=1)
    h = h.reshape(batch * length, d_model)

    lb_all = jnp.cumsum(jax.nn.softmax(hgrn_lb_raw.astype(F32), axis=0), axis=0)
    lb_all = lb_all - lb_all[0:1]
    q_scale = LOG2E / math.sqrt(HALF)

    for layer in range(depth):
        u = rmsnorm(h, norm1_g[layer])
        proj3 = matmul([u], w_in, layer, F32).reshape(batch, length, -1)
        o_a = hgrn2(proj3, lb_all[layer], hgrn_out_g[layer], width)
        qn, kn, vt = qkv_prep(proj3, (q_norm_g[layer].reshape(1, HEAD) * q_scale).astype(F32),
                              k_norm_g[layer].reshape(1, HEAD).astype(F32), width, tile)
        lp = diff_lambda[layer].astype(F32)
        lam_init = 0.8 - 0.6 * math.exp(-0.3 * layer)
        lam = jnp.exp(jnp.sum(lp[0] * lp[1])) - jnp.exp(jnp.sum(lp[2] * lp[3])) + lam_init
        o_b = diff_attention(qn, kn, vt, lam, lam_init, diff_out_g[layer], tile)
        h, u, row_ssq = matmul([o_a, o_b.reshape(batch * length, width)], w_out, layer, F32,
                               epilogue="residual", residual=h, next_gain=norm2_g[layer])
        z = matmul([u], w_mlp_up, layer, BF16, epilogue="relu2", row_ssq=row_ssq)
        h = matmul_residual_kgrid(z, w_mlp_down, layer, h)
    return h.reshape(batch, length, d_model)[:, PAD + N_META:]
```

```python
import functools
import math

import numpy as np
import jax
import jax.numpy as jnp
from jax import lax
from jax.experimental import pallas as pl
from jax.experimental.pallas import tpu as pltpu

N_META = 16
ATTN_CHUNK = 64
HEAD = 128
HALF = HEAD // 2
META_ROWS = 128
PAD = META_ROWS - N_META
HGRN_CHUNK = 128
HGRN_LEVELS = 7
ONES_ROWS = 16
ATTN_HEADS_PER_STEP = 4
EPS = 1e-6
MASK_VALUE = -1e30
LOG2E = math.log2(math.e)
V7X_VMEM_LIMIT_BYTES = 56 * 1024 * 1024
DENSE_WINDOW_BUDGET_BYTES = 50 * 1024 * 1024

F32 = jnp.float32
BF16 = jnp.bfloat16


def _params(semantics):
    return pltpu.CompilerParams(dimension_semantics=semantics, vmem_limit_bytes=V7X_VMEM_LIMIT_BYTES)


def _pick(n, candidates):
    for c in candidates:
        if n % c == 0:
            return c
    raise ValueError(f"no tile for {n} among {candidates}")


def _rmsnorm_kernel(x_ref, g_ref, o_ref):
    x = x_ref[...]
    ms = jnp.mean(x * x, axis=-1, keepdims=True)
    o_ref[...] = (x * lax.rsqrt(ms + EPS) * g_ref[...]).astype(o_ref.dtype)


def rmsnorm(x, g):
    m, d = x.shape
    bm = _pick(m, (512, 256, 128))
    return pl.pallas_call(
        _rmsnorm_kernel,
        grid=(m // bm,),
        in_specs=[pl.BlockSpec((bm, d), lambda i: (i, 0)), pl.BlockSpec((1, d), lambda i: (0, 0))],
        out_specs=pl.BlockSpec((bm, d), lambda i: (i, 0)),
        out_shape=jax.ShapeDtypeStruct((m, d), BF16),
        compiler_params=_params(("parallel",)),
        name="rmsnorm",
    )(x, g.reshape(1, d).astype(F32))


def _epilogue(acc, epilogue, r_ref):
    if epilogue == "relu2":
        acc = jnp.square(jnp.maximum(acc, 0.0))
    if epilogue == "residual":
        acc = acc + r_ref[...]
    return acc


def _matmul_kernel(*refs, n_parts, epilogue, with_meta):
    refs = list(refs)
    take = lambda n: [refs.pop(0) for _ in range(n)]
    xs, ws = take(n_parts), take(n_parts)
    r_ref = refs.pop(0) if epilogue == "residual" else None
    xms = take(n_parts) if with_meta else []
    rm_ref = refs.pop(0) if with_meta and epilogue == "residual" else None
    o_ref = refs.pop(0)

    def product(x_refs, res_ref):
        acc = jnp.dot(x_refs[0][...], ws[0][...].astype(BF16), preferred_element_type=F32)
        for x_ref, w_ref in zip(x_refs[1:], ws[1:]):
            acc = acc + jnp.dot(x_ref[...], w_ref[...].astype(BF16), preferred_element_type=F32)
        return _epilogue(acc, epilogue, res_ref)

    o_ref[...] = product(xs, r_ref).astype(o_ref.dtype)
    if with_meta:
        om_ref = refs.pop(0)

        @pl.when(pl.program_id(0) == 0)
        def _():
            om_ref[...] = product(xms, rm_ref).astype(om_ref.dtype)

        @pl.when(pl.program_id(0) != 0)
        def _():
            om_ref[...] = jnp.zeros_like(om_ref)


def _meta_out(row_tiles, n, bn, dtype, index_map):
    return (pl.BlockSpec((META_ROWS, bn), index_map),
            jax.ShapeDtypeStruct((row_tiles * META_ROWS, n), dtype))


def matmul(xs, w_stack, layer, out_dtype, epilogue=None, residual=None, meta_xs=None, meta_residual=None):
    m = xs[0].shape[0]
    n = w_stack.shape[2]
    d_in = sum(x.shape[1] for x in xs)
    with_meta = meta_xs is not None
    bn = _pick(n, (512, 256, 128))

    def vmem_bytes(bm):
        per_step = bm * d_in * 2 + d_in * bn * 4 + bm * bn * jnp.dtype(out_dtype).itemsize
        if epilogue == "residual":
            per_step += bm * bn * 4
        return 2 * per_step

    bm = next(c for c in (2048, 1024, 512, 256, 128)
              if m % c == 0 and vmem_bytes(c) <= DENSE_WINDOW_BUDGET_BYTES)
    in_specs = [pl.BlockSpec((bm, x.shape[1]), lambda i, j: (i, 0)) for x in xs]
    row_block = 0
    for x in xs:
        kp = x.shape[1]
        in_specs.append(pl.BlockSpec((None, kp, bn), functools.partial(
            lambda i, j, rb: (layer, rb, j), rb=row_block // kp)))
        row_block += kp
    args = list(xs) + [w_stack] * len(xs)
    if epilogue == "residual":
        in_specs.append(pl.BlockSpec((bm, bn), lambda i, j: (i, j)))
        args.append(residual)
    out_specs = pl.BlockSpec((bm, bn), lambda i, j: (i, j))
    out_shape = jax.ShapeDtypeStruct((m, n), out_dtype)
    if with_meta:
        in_specs += [pl.BlockSpec((META_ROWS, x.shape[1]), lambda i, j: (0, 0)) for x in meta_xs]
        args += list(meta_xs)
        if epilogue == "residual":
            in_specs.append(pl.BlockSpec((META_ROWS, bn), lambda i, j: (0, j)))
            args.append(meta_residual)
        mspec, mshape = _meta_out(m // bm, n, bn, out_dtype, lambda i, j: (i, j))
        out_specs, out_shape = [out_specs, mspec], [out_shape, mshape]
    out = pl.pallas_call(
        functools.partial(_matmul_kernel, n_parts=len(xs), epilogue=epilogue, with_meta=with_meta),
        grid=(m // bm, n // bn),
        in_specs=in_specs,
        out_specs=out_specs,
        out_shape=out_shape,
        compiler_params=_params(("parallel", "parallel")),
        name="matmul_" + (epilogue or "plain"),
    )(*args)
    return (out[0], out[1][:META_ROWS]) if with_meta else out


def _matmul_kgrid_kernel(*refs, with_meta):
    if with_meta:
        x_ref, w_ref, r_ref, xm_ref, rm_ref, o_ref, om_ref, acc_ref, accm_ref = refs
    else:
        x_ref, w_ref, r_ref, o_ref, acc_ref = refs
    i, k = pl.program_id(0), pl.program_id(2)
    last = pl.num_programs(2) - 1

    @pl.when(k == 0)
    def _():
        acc_ref[...] = r_ref[...]

    acc_ref[...] += jnp.dot(x_ref[...], w_ref[...].astype(BF16), preferred_element_type=F32)

    @pl.when(k == last)
    def _():
        o_ref[...] = acc_ref[...]

    if with_meta:
        @pl.when((i == 0) & (k == 0))
        def _():
            accm_ref[...] = rm_ref[...]

        @pl.when(i == 0)
        def _():
            accm_ref[...] += jnp.dot(xm_ref[...], w_ref[...].astype(BF16), preferred_element_type=F32)

        @pl.when((i == 0) & (k == last))
        def _():
            om_ref[...] = accm_ref[...]

        @pl.when((i != 0) & (k == last))
        def _():
            om_ref[...] = jnp.zeros_like(om_ref)


def matmul_residual_kgrid(x, w_stack, layer, residual, meta_x=None, meta_residual=None):
    m, kdim = x.shape
    n = w_stack.shape[2]
    with_meta = meta_x is not None
    bm = _pick(m, (1024, 512, 256, 128))
    bn = _pick(n, (1024, 512, 256, 128))
    bk = _pick(kdim, (2048, 1024, 512, 256, 128))
    tile = pl.BlockSpec((bm, bn), lambda i, j, k: (i, j))
    in_specs = [
        pl.BlockSpec((bm, bk), lambda i, j, k: (i, k)),
        pl.BlockSpec((None, bk, bn), lambda i, j, k: (layer, k, j)),
        tile,
    ]
    args = [x, w_stack, residual]
    out_specs, out_shape = tile, jax.ShapeDtypeStruct((m, n), F32)
    scratch = [pltpu.VMEM((bm, bn), F32)]
    if with_meta:
        in_specs += [pl.BlockSpec((META_ROWS, bk), lambda i, j, k: (0, k)),
                     pl.BlockSpec((META_ROWS, bn), lambda i, j, k: (0, j))]
        args += [meta_x, meta_residual]
        mspec, mshape = _meta_out(m // bm, n, bn, F32, lambda i, j, k: (i, j))
        out_specs, out_shape = [out_specs, mspec], [out_shape, mshape]
        scratch.append(pltpu.VMEM((META_ROWS, bn), F32))
    out = pl.pallas_call(
        functools.partial(_matmul_kgrid_kernel, with_meta=with_meta),
        grid=(m // bm, n // bn, kdim // bk),
        in_specs=in_specs,
        out_specs=out_specs,
        out_shape=out_shape,
        scratch_shapes=scratch,
        compiler_params=_params(("parallel", "parallel", "arbitrary")),
        name="matmul_residual_kgrid",
    )(*args)
    return (out[0], out[1][:META_ROWS]) if with_meta else out


def _qkv_prep_kernel(q_ref, k_ref, v_ref, gq_ref, gk_ref, pool_ref, qo_ref, ko_ref, vt_ref):
    pool = pool_ref[...]
    width = q_ref.shape[2]
    for h in range(width // HEAD):
        cols = slice(h * HEAD, (h + 1) * HEAD)
        for src, g_ref, dst in ((q_ref, gq_ref, qo_ref), (k_ref, gk_ref, ko_ref)):
            x = src[0, :, cols]
            ms = jnp.dot((x * x).astype(BF16), pool, preferred_element_type=F32)
            dst[0, :, cols] = (x * lax.rsqrt(ms + EPS) * g_ref[...]).astype(dst.dtype)
        vt_ref[0, h, 0] = v_ref[0, :, cols].T.astype(vt_ref.dtype)


def qkv_prep(proj3, q_gain, k_gain, width, tile):
    batch, length, _ = proj3.shape
    n_heads = width // HEAD
    nblk = length // tile
    pool = np.kron(np.eye(2), np.full((HALF, HALF), 1.0 / HALF)).astype(np.float32)
    qk = jax.ShapeDtypeStruct((batch, length, width), BF16)
    vt = jax.ShapeDtypeStruct((batch, n_heads, nblk, HEAD, tile), BF16)
    gspec = pl.BlockSpec((1, HEAD), lambda b, i: (0, 0))
    qkspec = pl.BlockSpec((1, tile, width), lambda b, i: (b, i, 0))
    return pl.pallas_call(
        _qkv_prep_kernel,
        grid=(batch, nblk),
        in_specs=[
            pl.BlockSpec((1, tile, width), lambda b, i: (b, i, 4)),
            pl.BlockSpec((1, tile, width), lambda b, i: (b, i, 5)),
            pl.BlockSpec((1, tile, width), lambda b, i: (b, i, 6)),
            gspec, gspec,
            pl.BlockSpec((HEAD, HEAD), lambda b, i: (0, 0)),
        ],
        out_specs=[qkspec, qkspec,
                   pl.BlockSpec((1, n_heads, 1, HEAD, tile), lambda b, i: (b, 0, i, 0, 0))],
        out_shape=[qk, qk, vt],
        compiler_params=_params(("parallel", "parallel")),
        name="qkv_prep",
    )(proj3, proj3, proj3, q_gain, k_gain, jnp.asarray(pool, BF16))


def _alibi_slopes_log2(n_heads):
    return np.exp2(-8.0 * np.arange(1, n_heads + 1, dtype=np.float64) / n_heads) * LOG2E


def _attn_key_feat(n_heads, rows, mask_pad):
    ki = np.arange(rows)
    feat = (_alibi_slopes_log2(n_heads)[None, :] * ki[:, None]).astype(np.float32)
    if mask_pad:
        feat = np.where((ki < PAD)[:, None], np.float32(MASK_VALUE), feat)
    hi = feat.astype(BF16)
    lo = np.where(feat <= MASK_VALUE, np.float32(0.0), feat - hi.astype(np.float32)).astype(BF16)
    key_feat = np.zeros((rows, n_heads, HEAD), BF16)
    key_feat[..., 0] = hi
    key_feat[..., 1] = lo
    return jnp.asarray(key_feat.reshape(rows, n_heads * HEAD))


def _attn_diag_bias(n_heads, tile):
    slopes = jnp.asarray(_alibi_slopes_log2(n_heads), F32)
    ki = jnp.arange(tile, dtype=jnp.int32)
    kk, qq = ki[:, None], ki[None, :]
    allowed = (kk // ATTN_CHUNK) <= (qq // ATTN_CHUNK)
    rel = (-jnp.abs(qq - kk) + qq - kk).astype(F32)
    return jnp.where(allowed[None], slopes[:, None, None] * rel[None], MASK_VALUE)


def _ones_rows(cols):
    return jnp.broadcast_to((jnp.arange(ONES_ROWS) == 0).astype(BF16)[:, None], (ONES_ROWS, cols))


def _attn_kernel(*refs, tile, heads, with_meta, mask_pad):
    refs = list(refs)
    scal_ref, slope_ref, q_ref, k_ref, vt_ref, kfeat_ref, qfeat_ref, ones_ref, dbias_ref, g_ref = refs[:10]
    if with_meta:
        km_ref, vtm_ref, kfeatm_ref, onesm_ref = refs[10:14]
    o_ref, acc_ref, s_ref = refs[-3:]
    i = pl.program_id(2)
    lam = scal_ref[0]
    out_scale = scal_ref[1]
    qfeat = qfeat_ref[...]
    ones = ones_ref[...]
    lane = lax.broadcasted_iota(jnp.int32, (tile, HEAD), 1)

    def head_cols(h):
        return slice(h * HEAD, (h + 1) * HEAD)

    qcat = []
    for h in range(heads):
        q = q_ref[0, :, head_cols(h)]
        qcat.append(jnp.concatenate(
            [jnp.concatenate([jnp.where(keep, q, jnp.zeros_like(q)), qfeat], axis=1)
             for keep in (lane < HALF, lane >= HALF)], axis=0))
    slope = [slope_ref[pl.program_id(0) * heads + h] for h in range(heads)]
    block_shift = [s * float(tile) for s in slope]

    def scores(kb, feat, h):
        kaug = jnp.concatenate([kb, feat], axis=1)
        return lax.dot_general(kaug, qcat[h], (((1,), (1,)), ((), ())), preferred_element_type=F32)

    def produce(h, j, slot, diag=False):
        kb = k_ref[0, pl.ds(pl.multiple_of(j * tile, tile), tile), head_cols(h)]
        s = scores(kb, kfeat_ref[:, head_cols(h)], h)
        if diag:
            s = s + jnp.concatenate([dbias_ref[h]] * 2, axis=1)
        s_ref[h, slot] = s
        return jnp.max(s, axis=0, keepdims=True)

    def consume(h, j, slot, col_max, m_old):
        vaug = jnp.concatenate([vt_ref[0, h, j], ones], axis=0)
        shift = block_shift[h] * (j - i).astype(F32)
        m_new = jnp.maximum(m_old, col_max + shift)
        p = jnp.exp2(s_ref[h, slot] - (m_new - shift)).astype(BF16)
        acc_ref[h] = acc_ref[h] * jnp.exp2(m_old - m_new) + jnp.dot(vaug, p, preferred_element_type=F32)
        return m_new

    def start(h):
        if not with_meta:
            acc_ref[h] = jnp.zeros(acc_ref.shape[1:], F32)
            return jnp.full((1, 2 * tile), MASK_VALUE, F32)
        s = scores(km_ref[0, :, head_cols(h)], kfeatm_ref[:, head_cols(h)], h)
        shift = -(slope[h] * float(META_ROWS) + block_shift[h] * i.astype(F32))
        m_new = jnp.max(s, axis=0, keepdims=True) + shift
        p = jnp.exp2(s - (m_new - shift)).astype(BF16)
        vaug = jnp.concatenate([vtm_ref[0, h, 0], onesm_ref[...]], axis=0)
        acc_ref[h] = jnp.dot(vaug, p, preferred_element_type=F32)
        return m_new

    init = tuple((start(h), produce(h, i, 0, diag=True)) for h in range(heads))

    def pipe(t, slot, carry):
        prev = jnp.where(t == 0, i, t - 1)
        return tuple((consume(h, prev, slot, carry[h][1], carry[h][0]), produce(h, t, 1 - slot))
                     for h in range(heads))

    carry = lax.fori_loop(0, i // 2, lambda n, c: pipe(2 * n + 1, 1, pipe(2 * n, 0, c)), init)

    def odd_tail(c):
        c = pipe(i - 1, 0, c)
        for h in range(heads):
            consume(h, i - 1, 1, c[h][1], c[h][0])
        return 0

    def even_tail(c):
        for h in range(heads):
            consume(h, jnp.where(i == 0, i, i - 1), 0, c[h][1], c[h][0])
        return 0

    lax.cond(i % 2 == 1, odd_tail, even_tail, carry)

    row = lax.broadcasted_iota(jnp.int32, (tile, HEAD), 0) + i * tile
    for h in range(heads):
        def normalized(c):
            cols = slice(c * tile, (c + 1) * tile)
            return acc_ref[h, :HEAD, cols] / acc_ref[h, HEAD:HEAD + 1, cols]

        o = normalized(0) - lam * normalized(1)
        o = o * lax.rsqrt(jnp.mean(o * o, axis=0, keepdims=True) + EPS)
        ot = o.T * (g_ref[...] * out_scale)
        if mask_pad:
            ot = jnp.where(row >= PAD, ot, 0.0)
        o_ref[0, :, head_cols(h)] = ot.astype(o_ref.dtype)


def diff_attention(qn, kn, vt, lam, lam_init, out_g, tile, meta=None):
    batch, length, width = qn.shape
    n_heads = width // HEAD
    heads = _pick(n_heads, (ATTN_HEADS_PER_STEP, 1))
    nq = length // tile
    with_meta = meta is not None
    scal = jnp.stack([lam.astype(F32), jnp.asarray(1.0 - lam_init, F32)])
    query_feat = jnp.broadcast_to((jnp.arange(HEAD) < 2).astype(BF16)[None, :], (tile, HEAD))
    smem = pl.BlockSpec(memory_space=pltpu.SMEM)
    bw = heads * HEAD
    in_specs = [
        smem, smem,
        pl.BlockSpec((1, tile, bw), lambda g, b, i: (b, i, g)),
        pl.BlockSpec((1, length, bw), lambda g, b, i: (b, 0, g)),
        pl.BlockSpec((1, heads, nq, HEAD, tile), lambda g, b, i: (b, g, 0, 0, 0)),
        pl.BlockSpec((tile, bw), lambda g, b, i: (0, g)),
        pl.BlockSpec((tile, HEAD), lambda g, b, i: (0, 0)),
        pl.BlockSpec((ONES_ROWS, tile), lambda g, b, i: (0, 0)),
        pl.BlockSpec((heads, tile, tile), lambda g, b, i: (g, 0, 0)),
        pl.BlockSpec((1, HEAD), lambda g, b, i: (0, 0)),
    ]
    args = [scal, jnp.asarray(_alibi_slopes_log2(n_heads), F32), qn, kn, vt,
            _attn_key_feat(n_heads, tile, mask_pad=not with_meta), query_feat, _ones_rows(tile),
            _attn_diag_bias(n_heads, tile), out_g.reshape(1, HEAD).astype(F32)]
    if with_meta:
        in_specs += [
            pl.BlockSpec((1, META_ROWS, bw), lambda g, b, i: (0, 0, g)),
            pl.BlockSpec((1, heads, 1, HEAD, META_ROWS), lambda g, b, i: (0, g, 0, 0, 0)),
            pl.BlockSpec((META_ROWS, bw), lambda g, b, i: (0, g)),
            pl.BlockSpec((ONES_ROWS, META_ROWS), lambda g, b, i: (0, 0)),
        ]
        args += [meta[0], meta[1], _attn_key_feat(n_heads, META_ROWS, mask_pad=True), _ones_rows(META_ROWS)]
    return pl.pallas_call(
        functools.partial(_attn_kernel, tile=tile, heads=heads, with_meta=with_meta, mask_pad=not with_meta),
        grid=(n_heads // heads, batch, nq),
        in_specs=in_specs,
        out_specs=pl.BlockSpec((1, tile, bw), lambda g, b, i: (b, i, g)),
        out_shape=jax.ShapeDtypeStruct((batch, length, width), BF16),
        scratch_shapes=[pltpu.VMEM((heads, HEAD + ONES_ROWS, 2 * tile), F32),
                        pltpu.VMEM((heads, 2, tile, 2 * tile), F32)],
        compiler_params=_params(("parallel", "parallel", "parallel")),
        name="diff_attention",
    )(*args)


def _hgrn_constants():
    c = HGRN_CHUNK
    t = np.arange(c)[:, None]
    r = np.arange(c)[None, :]
    sums = []
    pair = [np.eye(c)]
    for level in range(1, HGRN_LEVELS + 1):
        size = 2 ** level
        mid = (t // size) * size + size // 2 - 1
        upper = (t % size) >= size // 2
        sums.append(np.where(upper, (r > mid) & (r <= t), (r > t) & (r <= mid)))
        same = (t // size) == (r // size)
        pair.append(same & upper & ((r % size) < size // 2))
    sums.append(r <= t)
    sums.append(r > t)
    sums = np.concatenate(sums, axis=0).astype(np.float32)
    return (jnp.asarray(np.concatenate([sums, sums], axis=1), BF16),
            jnp.asarray(np.stack(pair).astype(np.float32)))


def _hgrn_kernel(*refs, heads, chunks, with_init, emit_state):
    refs = list(refs)
    q_ref, f_ref, v_ref, g_ref, lb_ref, og_ref, sums_ref, pair_ref = refs[:8]
    init_ref = refs[8] if with_init else None
    o_ref = refs[8 + with_init]
    state_out_ref = refs[9 + with_init] if emit_state else None
    state_ref = refs[-1]
    c = HGRN_CHUNK

    @pl.when(pl.program_id(2) == 0)
    def _():
        state_ref[...] = init_ref[...] if with_init else jnp.zeros_like(state_ref)

    def chunk_body(n):
        rows = slice(n * c, (n + 1) * c)
        lb_all = lb_ref[...]
        sig_all = jax.nn.sigmoid(f_ref[0, rows, :])
        logf = jnp.log2(lb_all + (1.0 - lb_all) * sig_all)
        k_all = ((1.0 - lb_all) * (1.0 - sig_all)).astype(BF16)
        hi = logf.astype(BF16)
        lo = (logf - hi.astype(F32)).astype(BF16)
        decay_all = jnp.exp2(jnp.dot(sums_ref[...], jnp.concatenate([hi, lo], axis=0),
                                     preferred_element_type=F32))
        for h in range(heads):
            cols = slice(h * HEAD, (h + 1) * HEAD)
            q = q_ref[0, rows, cols].astype(BF16)
            v = v_ref[0, rows, cols].astype(BF16)
            gate = g_ref[0, rows, cols]
            k = k_all[:, cols]
            decay = decay_all[:, cols]

            a = pair_ref[0] * lax.dot_general(q, k, (((1,), (1,)), ((), ())), preferred_element_type=F32)
            for level in range(1, HGRN_LEVELS + 1):
                d = decay[(level - 1) * c:level * c].astype(BF16)
                a = a + pair_ref[level] * lax.dot_general(q * d, k * d, (((1,), (1,)), ((), ())),
                                                          preferred_element_type=F32)
            from_start = decay[7 * c:8 * c]
            to_end = decay[8 * c:9 * c].astype(BF16)
            state_t = state_ref[h]
            o = jnp.dot(a.astype(BF16), v, preferred_element_type=F32)
            o = o + lax.dot_general(q * from_start.astype(BF16), state_t.astype(BF16),
                                    (((1,), (1,)), ((), ())), preferred_element_type=F32)
            state_ref[h] = state_t * from_start[c - 1:c, :] + lax.dot_general(
                v, k * to_end, (((0,), (0,)), ((), ())), preferred_element_type=F32)

            y = o * lax.rsqrt(jnp.mean(o * o, axis=-1, keepdims=True) + EPS) * og_ref[...]
            o_ref[0, rows, cols] = (y * (gate * jax.nn.sigmoid(gate))).astype(o_ref.dtype)

    for n in range(chunks):
        chunk_body(n)

    if emit_state:
        @pl.when(pl.program_id(2) == pl.num_programs(2) - 1)
        def _():
            state_out_ref[0] = state_ref[...]


def hgrn2(proj3, lb, out_g, width, init_state=None, emit_state=False):
    batch, length, _ = proj3.shape
    n_heads = width // HEAD
    heads = _pick(n_heads, (4, 2, 1))
    tile = _pick(length, (512, 384, 256, 128))
    groups = n_heads // heads
    bw = heads * HEAD
    sums, pair = _hgrn_constants()
    with_init = init_state is not None

    def seg_spec(seg):
        return pl.BlockSpec((1, tile, bw), lambda b, hg, t: (b, t, seg * groups + hg))

    in_specs = [
        seg_spec(0), seg_spec(1), seg_spec(2), seg_spec(3),
        pl.BlockSpec((1, bw), lambda b, hg, t: (0, hg)),
        pl.BlockSpec((1, HEAD), lambda b, hg, t: (0, 0)),
        pl.BlockSpec(sums.shape, lambda b, hg, t: (0, 0)),
        pl.BlockSpec(pair.shape, lambda b, hg, t: (0, 0, 0)),
    ]
    args = [proj3, proj3, proj3, proj3, lb.reshape(1, width).astype(F32),
            out_g.reshape(1, HEAD).astype(F32), sums, pair]
    if with_init:
        in_specs.append(pl.BlockSpec((heads, HEAD, HEAD), lambda b, hg, t: (hg, 0, 0)))
        args.append(init_state)
    out_specs = pl.BlockSpec((1, tile, bw), lambda b, hg, t: (b, t, hg))
    out_shape = jax.ShapeDtypeStruct((batch, length, width), BF16)
    if emit_state:
        out_specs = [out_specs, pl.BlockSpec((1, heads, HEAD, HEAD), lambda b, hg, t: (b, hg, 0, 0))]
        out_shape = [out_shape, jax.ShapeDtypeStruct((batch, n_heads, HEAD, HEAD), F32)]
    return pl.pallas_call(
        functools.partial(_hgrn_kernel, heads=heads, chunks=tile // HGRN_CHUNK, with_init=with_init,
                          emit_state=emit_state),
        grid=(batch, groups, length // tile),
        in_specs=in_specs,
        out_specs=out_specs,
        out_shape=out_shape,
        scratch_shapes=[pltpu.VMEM((heads, HEAD, HEAD), F32)],
        compiler_params=_params(("parallel", "parallel", "arbitrary")),
        name="hgrn2",
    )(*args)


def kernel(x, meta_tokens, norm1_g, w_in, hgrn_lb_raw, hgrn_out_g, q_norm_g, k_norm_g, diff_lambda,
           diff_out_g, w_out, norm2_g, w_mlp_up, w_mlp_down):
    batch, seq, d_model = x.shape
    depth = w_in.shape[0]
    width = d_model // 2
    tile = _pick(seq, (512, 384, 256, 128))
    h = x.reshape(batch * seq, d_model)
    hm = jnp.concatenate([jnp.zeros((PAD, d_model), x.dtype), meta_tokens.astype(x.dtype)], axis=0)

    lb_all = jnp.cumsum(jax.nn.softmax(hgrn_lb_raw.astype(F32), axis=0), axis=0)
    lb_all = lb_all - lb_all[0:1]
    q_scale = LOG2E / math.sqrt(HALF)

    for layer in range(depth):
        keep_meta = layer + 1 < depth
        q_gain = (q_norm_g[layer].reshape(1, HEAD) * q_scale).astype(F32)
        k_gain = k_norm_g[layer].reshape(1, HEAD).astype(F32)
        lp = diff_lambda[layer].astype(F32)
        lam_init = 0.8 - 0.6 * math.exp(-0.3 * layer)
        lam = jnp.exp(jnp.sum(lp[0] * lp[1])) - jnp.exp(jnp.sum(lp[2] * lp[3])) + lam_init

        proj, projm = matmul([rmsnorm(h, norm1_g[layer])], w_in, layer, F32,
                             meta_xs=[rmsnorm(hm, norm1_g[layer])])
        proj3 = proj.reshape(batch, seq, -1)
        projm3 = projm.reshape(1, META_ROWS, -1)
        o_am, state = hgrn2(projm3, lb_all[layer], hgrn_out_g[layer], width, emit_state=True)
        o_a = hgrn2(proj3, lb_all[layer], hgrn_out_g[layer], width, init_state=state[0])
        qn_m, kn_m, vt_m = qkv_prep(projm3, q_gain, k_gain, width, META_ROWS)
        qn, kn, vt = qkv_prep(proj3, q_gain, k_gain, width, tile)
        o_b = diff_attention(qn, kn, vt, lam, lam_init, diff_out_g[layer], tile, meta=(kn_m, vt_m))
        mixed = [o_a.reshape(batch * seq, width), o_b.reshape(batch * seq, width)]
        if keep_meta:
            o_bm = diff_attention(qn_m, kn_m, vt_m, lam, lam_init, diff_out_g[layer], META_ROWS)
            h, hm = matmul(mixed, w_out, layer, F32, epilogue="residual", residual=h,
                           meta_xs=[o_am.reshape(META_ROWS, width), o_bm.reshape(META_ROWS, width)],
                           meta_residual=hm)
            z, zm = matmul([rmsnorm(h, norm2_g[layer])], w_mlp_up, layer, BF16, epilogue="relu2",
                           meta_xs=[rmsnorm(hm, norm2_g[layer])])
            h, hm = matmul_residual_kgrid(z, w_mlp_down, layer, h, meta_x=zm, meta_residual=hm)
        else:
            h = matmul(mixed, w_out, layer, F32, epilogue="residual", residual=h)
            z = matmul([rmsnorm(h, norm2_g[layer])], w_mlp_up, layer, BF16, epilogue="relu2")
            h = matmul_residual_kgrid(z, w_mlp_down, layer, h)
    return h.reshape(batch, seq, d_model)
```

```python
import functools
import math

import numpy as np
import jax
import jax.numpy as jnp
from jax import lax
from jax.experimental import pallas as pl
from jax.experimental.pallas import tpu as pltpu

N_META = 16
ATTN_CHUNK = 64
HEAD = 128
HALF = HEAD // 2
META_ROWS = 128
PAD = META_ROWS - N_META
HGRN_CHUNK = 128
HGRN_LEVELS = 7
ONES_ROWS = 16
ATTN_HEADS_PER_STEP = 4
EPS = 1e-6
MASK_VALUE = -1e30
LOG2E = math.log2(math.e)
V7X_VMEM_LIMIT_BYTES = 56 * 1024 * 1024
DENSE_WINDOW_BUDGET_BYTES = 50 * 1024 * 1024

F32 = jnp.float32
BF16 = jnp.bfloat16


def _params(semantics):
    return pltpu.CompilerParams(dimension_semantics=semantics, vmem_limit_bytes=V7X_VMEM_LIMIT_BYTES)


def _pick(n, candidates):
    for c in candidates:
        if n % c == 0:
            return c
    raise ValueError(f"no tile for {n} among {candidates}")


def _rmsnorm_kernel(x_ref, g_ref, o_ref):
    x = x_ref[...]
    ms = jnp.mean(x * x, axis=-1, keepdims=True)
    o_ref[...] = (x * lax.rsqrt(ms + EPS) * g_ref[...]).astype(o_ref.dtype)


def rmsnorm(x, g):
    m, d = x.shape
    bm = _pick(m, (512, 256, 128))
    return pl.pallas_call(
        _rmsnorm_kernel,
        grid=(m // bm,),
        in_specs=[pl.BlockSpec((bm, d), lambda i: (i, 0)), pl.BlockSpec((1, d), lambda i: (0, 0))],
        out_specs=pl.BlockSpec((bm, d), lambda i: (i, 0)),
        out_shape=jax.ShapeDtypeStruct((m, d), BF16),
        compiler_params=_params(("parallel",)),
        name="rmsnorm",
    )(x, g.reshape(1, d).astype(F32))


def _epilogue(acc, epilogue, r_ref):
    if epilogue == "relu2":
        acc = jnp.square(jnp.maximum(acc, 0.0))
    if epilogue == "residual":
        acc = acc + r_ref[...]
    return acc


def _matmul_kernel(*refs, n_parts, epilogue, with_meta):
    refs = list(refs)
    take = lambda n: [refs.pop(0) for _ in range(n)]
    xs, ws = take(n_parts), take(n_parts)
    r_ref = refs.pop(0) if epilogue == "residual" else None
    xms = take(n_parts) if with_meta else []
    rm_ref = refs.pop(0) if with_meta and epilogue == "residual" else None
    o_ref = refs.pop(0)

    def product(x_refs, res_ref):
        acc = jnp.dot(x_refs[0][...], ws[0][...].astype(BF16), preferred_element_type=F32)
        for x_ref, w_ref in zip(x_refs[1:], ws[1:]):
            acc = acc + jnp.dot(x_ref[...], w_ref[...].astype(BF16), preferred_element_type=F32)
        return _epilogue(acc, epilogue, res_ref)

    o_ref[...] = product(xs, r_ref).astype(o_ref.dtype)
    if with_meta:
        om_ref = refs.pop(0)

        @pl.when(pl.program_id(0) == 0)
        def _():
            om_ref[...] = product(xms, rm_ref).astype(om_ref.dtype)

        @pl.when(pl.program_id(0) != 0)
        def _():
            om_ref[...] = jnp.zeros_like(om_ref)


def _meta_out(row_tiles, n, bn, dtype, index_map):
    return (pl.BlockSpec((META_ROWS, bn), index_map),
            jax.ShapeDtypeStruct((row_tiles * META_ROWS, n), dtype))


def matmul(xs, w_stack, layer, out_dtype, epilogue=None, residual=None, meta_xs=None, meta_residual=None,
           single_buffer_rows=False):
    m = xs[0].shape[0]
    n = w_stack.shape[2]
    d_in = sum(x.shape[1] for x in xs)
    with_meta = meta_xs is not None
    bn = _pick(n, (512, 256, 128))

    row_buffers = 1 if single_buffer_rows else 2

    def vmem_bytes(bm):
        per_step = d_in * bn * 4 + bm * bn * jnp.dtype(out_dtype).itemsize
        if epilogue == "residual":
            per_step += bm * bn * 4
        return 2 * per_step + row_buffers * bm * d_in * 2

    bm = next(c for c in (2048, 1024, 512, 256, 128)
              if m % c == 0 and vmem_bytes(c) <= DENSE_WINDOW_BUDGET_BYTES)
    row_mode = dict(pipeline_mode=pl.Buffered(1)) if single_buffer_rows else {}
    in_specs = [pl.BlockSpec((bm, x.shape[1]), lambda i, j: (i, 0), **row_mode) for x in xs]
    row_block = 0
    for x in xs:
        kp = x.shape[1]
        in_specs.append(pl.BlockSpec((None, kp, bn), functools.partial(
            lambda i, j, rb: (layer, rb, j), rb=row_block // kp)))
        row_block += kp
    args = list(xs) + [w_stack] * len(xs)
    if epilogue == "residual":
        in_specs.append(pl.BlockSpec((bm, bn), lambda i, j: (i, j)))
        args.append(residual)
    out_specs = pl.BlockSpec((bm, bn), lambda i, j: (i, j))
    out_shape = jax.ShapeDtypeStruct((m, n), out_dtype)
    if with_meta:
        in_specs += [pl.BlockSpec((META_ROWS, x.shape[1]), lambda i, j: (0, 0)) for x in meta_xs]
        args += list(meta_xs)
        if epilogue == "residual":
            in_specs.append(pl.BlockSpec((META_ROWS, bn), lambda i, j: (0, j)))
            args.append(meta_residual)
        mspec, mshape = _meta_out(m // bm, n, bn, out_dtype, lambda i, j: (i, j))
        out_specs, out_shape = [out_specs, mspec], [out_shape, mshape]
    out = pl.pallas_call(
        functools.partial(_matmul_kernel, n_parts=len(xs), epilogue=epilogue, with_meta=with_meta),
        grid=(m // bm, n // bn),
        in_specs=in_specs,
        out_specs=out_specs,
        out_shape=out_shape,
        compiler_params=_params(("parallel", "parallel")),
        name="matmul_" + (epilogue or "plain"),
    )(*args)
    return (out[0], out[1][:META_ROWS]) if with_meta else out


def _matmul_kgrid_kernel(*refs, with_meta):
    if with_meta:
        x_ref, w_ref, r_ref, xm_ref, rm_ref, o_ref, om_ref, acc_ref, accm_ref = refs
    else:
        x_ref, w_ref, r_ref, o_ref, acc_ref = refs
    i, k = pl.program_id(0), pl.program_id(2)
    last = pl.num_programs(2) - 1

    @pl.when(k == 0)
    def _():
        acc_ref[...] = r_ref[...]

    acc_ref[...] += jnp.dot(x_ref[...], w_ref[...].astype(BF16), preferred_element_type=F32)

    @pl.when(k == last)
    def _():
        o_ref[...] = acc_ref[...]

    if with_meta:
        @pl.when((i == 0) & (k == 0))
        def _():
            accm_ref[...] = rm_ref[...]

        @pl.when(i == 0)
        def _():
            accm_ref[...] += jnp.dot(xm_ref[...], w_ref[...].astype(BF16), preferred_element_type=F32)

        @pl.when((i == 0) & (k == last))
        def _():
            om_ref[...] = accm_ref[...]

        @pl.when((i != 0) & (k == last))
        def _():
            om_ref[...] = jnp.zeros_like(om_ref)


def matmul_residual_kgrid(x, w_stack, layer, residual, meta_x=None, meta_residual=None):
    m, kdim = x.shape
    n = w_stack.shape[2]
    with_meta = meta_x is not None
    bm = _pick(m, (1024, 512, 256, 128))
    bn = _pick(n, (1024, 512, 256, 128))
    bk = _pick(kdim, (2048, 1024, 512, 256, 128))
    tile = pl.BlockSpec((bm, bn), lambda i, j, k: (i, j))
    in_specs = [
        pl.BlockSpec((bm, bk), lambda i, j, k: (i, k)),
        pl.BlockSpec((None, bk, bn), lambda i, j, k: (layer, k, j)),
        tile,
    ]
    args = [x, w_stack, residual]
    out_specs, out_shape = tile, jax.ShapeDtypeStruct((m, n), F32)
    scratch = [pltpu.VMEM((bm, bn), F32)]
    if with_meta:
        in_specs += [pl.BlockSpec((META_ROWS, bk), lambda i, j, k: (0, k)),
                     pl.BlockSpec((META_ROWS, bn), lambda i, j, k: (0, j))]
        args += [meta_x, meta_residual]
        mspec, mshape = _meta_out(m // bm, n, bn, F32, lambda i, j, k: (i, j))
        out_specs, out_shape = [out_specs, mspec], [out_shape, mshape]
        scratch.append(pltpu.VMEM((META_ROWS, bn), F32))
    out = pl.pallas_call(
        functools.partial(_matmul_kgrid_kernel, with_meta=with_meta),
        grid=(m // bm, n // bn, kdim // bk),
        in_specs=in_specs,
        out_specs=out_specs,
        out_shape=out_shape,
        scratch_shapes=scratch,
        compiler_params=_params(("parallel", "parallel", "arbitrary")),
        name="matmul_residual_kgrid",
    )(*args)
    return (out[0], out[1][:META_ROWS]) if with_meta else out


def _qkv_prep_kernel(q_ref, k_ref, v_ref, gq_ref, gk_ref, pool_ref, qo_ref, ko_ref, vt_ref):
    pool = pool_ref[...]
    width = q_ref.shape[2]
    for h in range(width // HEAD):
        cols = slice(h * HEAD, (h + 1) * HEAD)
        for src, g_ref, dst in ((q_ref, gq_ref, qo_ref), (k_ref, gk_ref, ko_ref)):
            x = src[0, :, cols].astype(F32)
            ms = jnp.dot((x * x).astype(BF16), pool, preferred_element_type=F32)
            dst[0, :, cols] = (x * lax.rsqrt(ms + EPS) * g_ref[...]).astype(dst.dtype)
        vt_ref[0, h, 0] = v_ref[0, :, cols].astype(F32).T.astype(vt_ref.dtype)


def qkv_prep(proj3, q_gain, k_gain, width, tile):
    batch, length, _ = proj3.shape
    n_heads = width // HEAD
    nblk = length // tile
    pool = np.kron(np.eye(2), np.full((HALF, HALF), 1.0 / HALF)).astype(np.float32)
    qk = jax.ShapeDtypeStruct((batch, length, width), BF16)
    vt = jax.ShapeDtypeStruct((batch, n_heads, nblk, HEAD, tile), BF16)
    gspec = pl.BlockSpec((1, HEAD), lambda b, i: (0, 0))
    qkspec = pl.BlockSpec((1, tile, width), lambda b, i: (b, i, 0))
    return pl.pallas_call(
        _qkv_prep_kernel,
        grid=(batch, nblk),
        in_specs=[
            pl.BlockSpec((1, tile, width), lambda b, i: (b, i, 4)),
            pl.BlockSpec((1, tile, width), lambda b, i: (b, i, 5)),
            pl.BlockSpec((1, tile, width), lambda b, i: (b, i, 6)),
            gspec, gspec,
            pl.BlockSpec((HEAD, HEAD), lambda b, i: (0, 0)),
        ],
        out_specs=[qkspec, qkspec,
                   pl.BlockSpec((1, n_heads, 1, HEAD, tile), lambda b, i: (b, 0, i, 0, 0))],
        out_shape=[qk, qk, vt],
        compiler_params=_params(("parallel", "parallel")),
        name="qkv_prep",
    )(proj3, proj3, proj3, q_gain, k_gain, jnp.asarray(pool, BF16))


def _alibi_slopes_log2(n_heads):
    return np.exp2(-8.0 * np.arange(1, n_heads + 1, dtype=np.float64) / n_heads) * LOG2E


def _attn_key_feat(n_heads, rows, mask_pad):
    ki = np.arange(rows)
    feat = (_alibi_slopes_log2(n_heads)[None, :] * ki[:, None]).astype(np.float32)
    if mask_pad:
        feat = np.where((ki < PAD)[:, None], np.float32(MASK_VALUE), feat)
    hi = feat.astype(BF16)
    lo = np.where(feat <= MASK_VALUE, np.float32(0.0), feat - hi.astype(np.float32)).astype(BF16)
    key_feat = np.zeros((rows, n_heads, HEAD), BF16)
    key_feat[..., 0] = hi
    key_feat[..., 1] = lo
    return jnp.asarray(key_feat.reshape(rows, n_heads * HEAD))


def _attn_diag_bias(n_heads, tile):
    slopes = jnp.asarray(_alibi_slopes_log2(n_heads), F32)
    ki = jnp.arange(tile, dtype=jnp.int32)
    kk, qq = ki[:, None], ki[None, :]
    allowed = (kk // ATTN_CHUNK) <= (qq // ATTN_CHUNK)
    rel = (-jnp.abs(qq - kk) + qq - kk).astype(F32)
    return jnp.where(allowed[None], slopes[:, None, None] * rel[None], MASK_VALUE)


def _ones_rows(cols):
    return jnp.broadcast_to((jnp.arange(ONES_ROWS) == 0).astype(BF16)[:, None], (ONES_ROWS, cols))


def _attn_kernel(*refs, tile, heads, with_meta, mask_pad):
    refs = list(refs)
    scal_ref, slope_ref, q_ref, k_ref, vt_ref, kfeat_ref, qfeat_ref, ones_ref, dbias_ref, g_ref = refs[:10]
    if with_meta:
        km_ref, vtm_ref, kfeatm_ref, onesm_ref = refs[10:14]
    o_ref, acc_ref, s_ref = refs[-3:]
    i = pl.program_id(2)
    lam = scal_ref[0]
    out_scale = scal_ref[1]
    qfeat = qfeat_ref[...]
    ones = ones_ref[...]
    lane = lax.broadcasted_iota(jnp.int32, (tile, HEAD), 1)

    def head_cols(h):
        return slice(h * HEAD, (h + 1) * HEAD)

    qcat = []
    for h in range(heads):
        q = q_ref[0, :, head_cols(h)]
        qcat.append(jnp.concatenate(
            [jnp.concatenate([jnp.where(keep, q, jnp.zeros_like(q)), qfeat], axis=1)
             for keep in (lane < HALF, lane >= HALF)], axis=0))
    slope = [slope_ref[pl.program_id(0) * heads + h] for h in range(heads)]
    block_shift = [s * float(tile) for s in slope]

    def scores(kb, feat, h):
        kaug = jnp.concatenate([kb, feat], axis=1)
        return lax.dot_general(kaug, qcat[h], (((1,), (1,)), ((), ())), preferred_element_type=F32)

    def produce(h, j, slot, diag=False):
        kb = k_ref[0, pl.ds(pl.multiple_of(j * tile, tile), tile), head_cols(h)]
        s = scores(kb, kfeat_ref[:, head_cols(h)], h)
        if diag:
            s = s + jnp.concatenate([dbias_ref[h]] * 2, axis=1)
        s_ref[h, slot] = s
        return jnp.max(s, axis=0, keepdims=True)

    def consume(h, j, slot, col_max, m_old):
        vaug = jnp.concatenate([vt_ref[0, h, j], ones], axis=0)
        shift = block_shift[h] * (j - i).astype(F32)
        m_new = jnp.maximum(m_old, col_max + shift)
        p = jnp.exp2(s_ref[h, slot] - (m_new - shift)).astype(BF16)
        acc_ref[h] = acc_ref[h] * jnp.exp2(m_old - m_new) + jnp.dot(vaug, p, preferred_element_type=F32)
        return m_new

    def start(h):
        if not with_meta:
            acc_ref[h] = jnp.zeros(acc_ref.shape[1:], F32)
            return jnp.full((1, 2 * tile), MASK_VALUE, F32)
        s = scores(km_ref[0, :, head_cols(h)], kfeatm_ref[:, head_cols(h)], h)
        shift = -(slope[h] * float(META_ROWS) + block_shift[h] * i.astype(F32))
        m_new = jnp.max(s, axis=0, keepdims=True) + shift
        p = jnp.exp2(s - (m_new - shift)).astype(BF16)
        vaug = jnp.concatenate([vtm_ref[0, h, 0], onesm_ref[...]], axis=0)
        acc_ref[h] = jnp.dot(vaug, p, preferred_element_type=F32)
        return m_new

    init = tuple((start(h), produce(h, i, 0, diag=True)) for h in range(heads))

    def pipe(t, slot, carry):
        prev = jnp.where(t == 0, i, t - 1)
        return tuple((consume(h, prev, slot, carry[h][1], carry[h][0]), produce(h, t, 1 - slot))
                     for h in range(heads))

    carry = lax.fori_loop(0, i // 2, lambda n, c: pipe(2 * n + 1, 1, pipe(2 * n, 0, c)), init)

    def odd_tail(c):
        c = pipe(i - 1, 0, c)
        for h in range(heads):
            consume(h, i - 1, 1, c[h][1], c[h][0])
        return 0

    def even_tail(c):
        for h in range(heads):
            consume(h, jnp.where(i == 0, i, i - 1), 0, c[h][1], c[h][0])
        return 0

    lax.cond(i % 2 == 1, odd_tail, even_tail, carry)

    row = lax.broadcasted_iota(jnp.int32, (tile, HEAD), 0) + i * tile
    for h in range(heads):
        def normalized(c):
            cols = slice(c * tile, (c + 1) * tile)
            return acc_ref[h, :HEAD, cols] / acc_ref[h, HEAD:HEAD + 1, cols]

        o = normalized(0) - lam * normalized(1)
        o = o * lax.rsqrt(jnp.mean(o * o, axis=0, keepdims=True) + EPS)
        ot = o.T * (g_ref[...] * out_scale)
        if mask_pad:
            ot = jnp.where(row >= PAD, ot, 0.0)
        o_ref[0, :, head_cols(h)] = ot.astype(o_ref.dtype)


def diff_attention(qn, kn, vt, lam, lam_init, out_g, tile, meta=None):
    batch, length, width = qn.shape
    n_heads = width // HEAD
    heads = _pick(n_heads, (ATTN_HEADS_PER_STEP, 1))
    nq = length // tile
    with_meta = meta is not None
    scal = jnp.stack([lam.astype(F32), jnp.asarray(1.0 - lam_init, F32)])
    query_feat = jnp.broadcast_to((jnp.arange(HEAD) < 2).astype(BF16)[None, :], (tile, HEAD))
    smem = pl.BlockSpec(memory_space=pltpu.SMEM)
    bw = heads * HEAD
    in_specs = [
        smem, smem,
        pl.BlockSpec((1, tile, bw), lambda g, b, i: (b, i, g)),
        pl.BlockSpec((1, length, bw), lambda g, b, i: (b, 0, g)),
        pl.BlockSpec((1, heads, nq, HEAD, tile), lambda g, b, i: (b, g, 0, 0, 0)),
        pl.BlockSpec((tile, bw), lambda g, b, i: (0, g)),
        pl.BlockSpec((tile, HEAD), lambda g, b, i: (0, 0)),
        pl.BlockSpec((ONES_ROWS, tile), lambda g, b, i: (0, 0)),
        pl.BlockSpec((heads, tile, tile), lambda g, b, i: (g, 0, 0)),
        pl.BlockSpec((1, HEAD), lambda g, b, i: (0, 0)),
    ]
    args = [scal, jnp.asarray(_alibi_slopes_log2(n_heads), F32), qn, kn, vt,
            _attn_key_feat(n_heads, tile, mask_pad=not with_meta), query_feat, _ones_rows(tile),
            _attn_diag_bias(n_heads, tile), out_g.reshape(1, HEAD).astype(F32)]
    if with_meta:
        in_specs += [
            pl.BlockSpec((1, META_ROWS, bw), lambda g, b, i: (0, 0, g)),
            pl.BlockSpec((1, heads, 1, HEAD, META_ROWS), lambda g, b, i: (0, g, 0, 0, 0)),
            pl.BlockSpec((META_ROWS, bw), lambda g, b, i: (0, g)),
            pl.BlockSpec((ONES_ROWS, META_ROWS), lambda g, b, i: (0, 0)),
        ]
        args += [meta[0], meta[1], _attn_key_feat(n_heads, META_ROWS, mask_pad=True), _ones_rows(META_ROWS)]
    return pl.pallas_call(
        functools.partial(_attn_kernel, tile=tile, heads=heads, with_meta=with_meta, mask_pad=not with_meta),
        grid=(n_heads // heads, batch, nq),
        in_specs=in_specs,
        out_specs=pl.BlockSpec((1, tile, bw), lambda g, b, i: (b, i, g)),
        out_shape=jax.ShapeDtypeStruct((batch, length, width), BF16),
        scratch_shapes=[pltpu.VMEM((heads, HEAD + ONES_ROWS, 2 * tile), F32),
                        pltpu.VMEM((heads, 2, tile, 2 * tile), F32)],
        compiler_params=_params(("parallel", "parallel", "parallel")),
        name="diff_attention",
    )(*args)


def _hgrn_constants():
    c = HGRN_CHUNK
    t = np.arange(c)[:, None]
    r = np.arange(c)[None, :]
    sums = []
    pair = [np.eye(c)]
    for level in range(1, HGRN_LEVELS + 1):
        size = 2 ** level
        mid = (t // size) * size + size // 2 - 1
        upper = (t % size) >= size // 2
        sums.append(np.where(upper, (r > mid) & (r <= t), (r > t) & (r <= mid)))
        same = (t // size) == (r // size)
        pair.append(same & upper & ((r % size) < size // 2))
    sums.append(r <= t)
    sums.append(r > t)
    sums = np.concatenate(sums, axis=0).astype(np.float32)
    return (jnp.asarray(np.concatenate([sums, sums], axis=1), BF16),
            jnp.asarray(np.stack(pair).astype(np.float32)))


def _hgrn_kernel(*refs, heads, chunks, with_init, emit_state):
    refs = list(refs)
    q_ref, f_ref, v_ref, g_ref, lb_ref, og_ref, sums_ref, pair_ref = refs[:8]
    init_ref = refs[8] if with_init else None
    o_ref = refs[8 + with_init]
    state_out_ref = refs[9 + with_init] if emit_state else None
    state_ref = refs[-1]
    c = HGRN_CHUNK

    @pl.when(pl.program_id(2) == 0)
    def _():
        state_ref[...] = init_ref[...] if with_init else jnp.zeros_like(state_ref)

    def chunk_body(n):
        rows = slice(n * c, (n + 1) * c)
        lb_all = lb_ref[...]
        sig_all = jax.nn.sigmoid(f_ref[0, rows, :].astype(F32))
        logf = jnp.log2(lb_all + (1.0 - lb_all) * sig_all)
        k_all = ((1.0 - lb_all) * (1.0 - sig_all)).astype(BF16)
        hi = logf.astype(BF16)
        lo = (logf - hi.astype(F32)).astype(BF16)
        decay_all = jnp.exp2(jnp.dot(sums_ref[...], jnp.concatenate([hi, lo], axis=0),
                                     preferred_element_type=F32))
        for h in range(heads):
            cols = slice(h * HEAD, (h + 1) * HEAD)
            q = q_ref[0, rows, cols].astype(BF16)
            v = v_ref[0, rows, cols].astype(BF16)
            gate = g_ref[0, rows, cols].astype(F32)
            k = k_all[:, cols]
            decay = decay_all[:, cols]

            a = pair_ref[0] * lax.dot_general(q, k, (((1,), (1,)), ((), ())), preferred_element_type=F32)
            for level in range(1, HGRN_LEVELS + 1):
                d = decay[(level - 1) * c:level * c].astype(BF16)
                a = a + pair_ref[level] * lax.dot_general(q * d, k * d, (((1,), (1,)), ((), ())),
                                                          preferred_element_type=F32)
            from_start = decay[7 * c:8 * c]
            to_end = decay[8 * c:9 * c].astype(BF16)
            state_t = state_ref[h]
            o = jnp.dot(a.astype(BF16), v, preferred_element_type=F32)
            o = o + lax.dot_general(q * from_start.astype(BF16), state_t.astype(BF16),
                                    (((1,), (1,)), ((), ())), preferred_element_type=F32)
            state_ref[h] = state_t * from_start[c - 1:c, :] + lax.dot_general(
                v, k * to_end, (((0,), (0,)), ((), ())), preferred_element_type=F32)

            y = o * lax.rsqrt(jnp.mean(o * o, axis=-1, keepdims=True) + EPS) * og_ref[...]
            o_ref[0, rows, cols] = (y * (gate * jax.nn.sigmoid(gate))).astype(o_ref.dtype)

    for n in range(chunks):
        chunk_body(n)

    if emit_state:
        @pl.when(pl.program_id(2) == pl.num_programs(2) - 1)
        def _():
            state_out_ref[0] = state_ref[...]


def hgrn2(proj3, lb, out_g, width, init_state=None, emit_state=False):
    batch, length, _ = proj3.shape
    n_heads = width // HEAD
    heads = _pick(n_heads, (4, 2, 1))
    tile = _pick(length, (512, 384, 256, 128))
    groups = n_heads // heads
    bw = heads * HEAD
    sums, pair = _hgrn_constants()
    with_init = init_state is not None

    def seg_spec(seg):
        return pl.BlockSpec((1, tile, bw), lambda b, hg, t: (b, t, seg * groups + hg))

    in_specs = [
        seg_spec(0), seg_spec(1), seg_spec(2), seg_spec(3),
        pl.BlockSpec((1, bw), lambda b, hg, t: (0, hg)),
        pl.BlockSpec((1, HEAD), lambda b, hg, t: (0, 0)),
        pl.BlockSpec(sums.shape, lambda b, hg, t: (0, 0)),
        pl.BlockSpec(pair.shape, lambda b, hg, t: (0, 0, 0)),
    ]
    args = [proj3, proj3, proj3, proj3, lb.reshape(1, width).astype(F32),
            out_g.reshape(1, HEAD).astype(F32), sums, pair]
    if with_init:
        in_specs.append(pl.BlockSpec((heads, HEAD, HEAD), lambda b, hg, t: (hg, 0, 0)))
        args.append(init_state)
    out_specs = pl.BlockSpec((1, tile, bw), lambda b, hg, t: (b, t, hg))
    out_shape = jax.ShapeDtypeStruct((batch, length, width), BF16)
    if emit_state:
        out_specs = [out_specs, pl.BlockSpec((1, heads, HEAD, HEAD), lambda b, hg, t: (b, hg, 0, 0))]
        out_shape = [out_shape, jax.ShapeDtypeStruct((batch, n_heads, HEAD, HEAD), F32)]
    return pl.pallas_call(
        functools.partial(_hgrn_kernel, heads=heads, chunks=tile // HGRN_CHUNK, with_init=with_init,
                          emit_state=emit_state),
        grid=(batch, groups, length // tile),
        in_specs=in_specs,
        out_specs=out_specs,
        out_shape=out_shape,
        scratch_shapes=[pltpu.VMEM((heads, HEAD, HEAD), F32)],
        compiler_params=_params(("parallel", "parallel", "arbitrary")),
        name="hgrn2",
    )(*args)


def kernel(x, meta_tokens, norm1_g, w_in, hgrn_lb_raw, hgrn_out_g, q_norm_g, k_norm_g, diff_lambda,
           diff_out_g, w_out, norm2_g, w_mlp_up, w_mlp_down):
    batch, seq, d_model = x.shape
    depth = w_in.shape[0]
    width = d_model // 2
    tile = _pick(seq, (512, 384, 256, 128))
    h = x.reshape(batch * seq, d_model)
    hm = jnp.concatenate([jnp.zeros((PAD, d_model), x.dtype), meta_tokens.astype(x.dtype)], axis=0)

    lb_all = jnp.cumsum(jax.nn.softmax(hgrn_lb_raw.astype(F32), axis=0), axis=0)
    lb_all = lb_all - lb_all[0:1]
    q_scale = LOG2E / math.sqrt(HALF)

    for layer in range(depth):
        keep_meta = layer + 1 < depth
        q_gain = (q_norm_g[layer].reshape(1, HEAD) * q_scale).astype(F32)
        k_gain = k_norm_g[layer].reshape(1, HEAD).astype(F32)
        lp = diff_lambda[layer].astype(F32)
        lam_init = 0.8 - 0.6 * math.exp(-0.3 * layer)
        lam = jnp.exp(jnp.sum(lp[0] * lp[1])) - jnp.exp(jnp.sum(lp[2] * lp[3])) + lam_init

        proj, projm = matmul([rmsnorm(h, norm1_g[layer])], w_in, layer, BF16,
                             meta_xs=[rmsnorm(hm, norm1_g[layer])], single_buffer_rows=True)
        proj3 = proj.reshape(batch, seq, -1)
        projm3 = projm.reshape(1, META_ROWS, -1)
        o_am, state = hgrn2(projm3, lb_all[layer], hgrn_out_g[layer], width, emit_state=True)
        o_a = hgrn2(proj3, lb_all[layer], hgrn_out_g[layer], width, init_state=state[0])
        qn_m, kn_m, vt_m = qkv_prep(projm3, q_gain, k_gain, width, META_ROWS)
        qn, kn, vt = qkv_prep(proj3, q_gain, k_gain, width, tile)
        o_b = diff_attention(qn, kn, vt, lam, lam_init, diff_out_g[layer], tile, meta=(kn_m, vt_m))
        mixed = [o_a.reshape(batch * seq, width), o_b.reshape(batch * seq, width)]
        if keep_meta:
            o_bm = diff_attention(qn_m, kn_m, vt_m, lam, lam_init, diff_out_g[layer], META_ROWS)
            h, hm = matmul(mixed, w_out, layer, F32, epilogue="residual", residual=h,
                           meta_xs=[o_am.reshape(META_ROWS, width), o_bm.reshape(META_ROWS, width)],
                           meta_residual=hm)
            z, zm = matmul([rmsnorm(h, norm2_g[layer])], w_mlp_up, layer, BF16, epilogue="relu2",
                           meta_xs=[rmsnorm(hm, norm2_g[layer])], single_buffer_rows=True)
            h, hm = matmul_residual_kgrid(z, w_mlp_down, layer, h, meta_x=zm, meta_residual=hm)
        else:
            h = matmul(mixed, w_out, layer, F32, epilogue="residual", residual=h)
            z = matmul([rmsnorm(h, norm2_g[layer])], w_mlp_up, layer, BF16, epilogue="relu2",
                       single_buffer_rows=True)
            h = matmul_residual_kgrid(z, w_mlp_down, layer, h)
    return h.reshape(batch, seq, d_model)
```

```python
import functools
import math

import numpy as np
import jax
import jax.numpy as jnp
from jax import lax
from jax.experimental import pallas as pl
from jax.experimental.pallas import tpu as pltpu

N_META = 16
ATTN_CHUNK = 64
HEAD = 128
HALF = HEAD // 2
META_ROWS = 128
PAD = META_ROWS - N_META
HGRN_CHUNK = 128
HGRN_LEVELS = 7
ONES_ROWS = 16
ATTN_HEADS_PER_STEP = 4
EPS = 1e-6
MASK_VALUE = -1e30
LOG2E = math.log2(math.e)
V7X_VMEM_LIMIT_BYTES = 56 * 1024 * 1024
DENSE_WINDOW_BUDGET_BYTES = 50 * 1024 * 1024

F32 = jnp.float32
BF16 = jnp.bfloat16


def _params(semantics):
    return pltpu.CompilerParams(dimension_semantics=semantics, vmem_limit_bytes=V7X_VMEM_LIMIT_BYTES)


def _pick(n, candidates):
    for c in candidates:
        if n % c == 0:
            return c
    raise ValueError(f"no tile for {n} among {candidates}")


def _rmsnorm_kernel(x_ref, g_ref, o_ref):
    x = x_ref[...]
    ms = jnp.mean(x * x, axis=-1, keepdims=True)
    o_ref[...] = (x * lax.rsqrt(ms + EPS) * g_ref[...]).astype(o_ref.dtype)


def rmsnorm(x, g):
    m, d = x.shape
    bm = _pick(m, (512, 256, 128))
    return pl.pallas_call(
        _rmsnorm_kernel,
        grid=(m // bm,),
        in_specs=[pl.BlockSpec((bm, d), lambda i: (i, 0)), pl.BlockSpec((1, d), lambda i: (0, 0))],
        out_specs=pl.BlockSpec((bm, d), lambda i: (i, 0)),
        out_shape=jax.ShapeDtypeStruct((m, d), BF16),
        compiler_params=_params(("parallel",)),
        name="rmsnorm",
    )(x, g.reshape(1, d).astype(F32))


def _epilogue(acc, epilogue, r_ref):
    if epilogue == "relu2":
        acc = jnp.square(jnp.maximum(acc, 0.0))
    if epilogue == "residual":
        acc = acc + r_ref[...]
    return acc


def _matmul_kernel(*refs, n_parts, epilogue, with_meta):
    refs = list(refs)
    take = lambda n: [refs.pop(0) for _ in range(n)]
    xs, ws = take(n_parts), take(n_parts)
    r_ref = refs.pop(0) if epilogue == "residual" else None
    xms = take(n_parts) if with_meta else []
    rm_ref = refs.pop(0) if with_meta and epilogue == "residual" else None
    o_ref = refs.pop(0)

    def product(x_refs, res_ref):
        acc = jnp.dot(x_refs[0][...], ws[0][...].astype(BF16), preferred_element_type=F32)
        for x_ref, w_ref in zip(x_refs[1:], ws[1:]):
            acc = acc + jnp.dot(x_ref[...], w_ref[...].astype(BF16), preferred_element_type=F32)
        return _epilogue(acc, epilogue, res_ref)

    o_ref[...] = product(xs, r_ref).astype(o_ref.dtype)
    if with_meta:
        om_ref = refs.pop(0)

        @pl.when(pl.program_id(0) == 0)
        def _():
            om_ref[...] = product(xms, rm_ref).astype(om_ref.dtype)

        @pl.when(pl.program_id(0) != 0)
        def _():
            om_ref[...] = jnp.zeros_like(om_ref)


def _meta_out(row_tiles, n, bn, dtype, index_map):
    return (pl.BlockSpec((META_ROWS, bn), index_map),
            jax.ShapeDtypeStruct((row_tiles * META_ROWS, n), dtype))


def matmul(xs, w_stack, layer, out_dtype, epilogue=None, residual=None, meta_xs=None, meta_residual=None,
           single_buffer_rows=False):
    m = xs[0].shape[0]
    n = w_stack.shape[2]
    d_in = sum(x.shape[1] for x in xs)
    with_meta = meta_xs is not None
    bn = _pick(n, (512, 256, 128))

    row_buffers = 1 if single_buffer_rows else 2

    def vmem_bytes(bm):
        per_step = d_in * bn * 4 + bm * bn * jnp.dtype(out_dtype).itemsize
        if epilogue == "residual":
            per_step += bm * bn * 4
        return 2 * per_step + row_buffers * bm * d_in * 2

    bm = next(c for c in (2048, 1024, 512, 256, 128)
              if m % c == 0 and vmem_bytes(c) <= DENSE_WINDOW_BUDGET_BYTES)
    row_mode = dict(pipeline_mode=pl.Buffered(1)) if single_buffer_rows else {}
    in_specs = [pl.BlockSpec((bm, x.shape[1]), lambda i, j: (i, 0), **row_mode) for x in xs]
    row_block = 0
    for x in xs:
        kp = x.shape[1]
        in_specs.append(pl.BlockSpec((None, kp, bn), functools.partial(
            lambda i, j, rb: (layer, rb, j), rb=row_block // kp)))
        row_block += kp
    args = list(xs) + [w_stack] * len(xs)
    if epilogue == "residual":
        in_specs.append(pl.BlockSpec((bm, bn), lambda i, j: (i, j)))
        args.append(residual)
    out_specs = pl.BlockSpec((bm, bn), lambda i, j: (i, j))
    out_shape = jax.ShapeDtypeStruct((m, n), out_dtype)
    if with_meta:
        in_specs += [pl.BlockSpec((META_ROWS, x.shape[1]), lambda i, j: (0, 0)) for x in meta_xs]
        args += list(meta_xs)
        if epilogue == "residual":
            in_specs.append(pl.BlockSpec((META_ROWS, bn), lambda i, j: (0, j)))
            args.append(meta_residual)
        mspec, mshape = _meta_out(m // bm, n, bn, out_dtype, lambda i, j: (i, j))
        out_specs, out_shape = [out_specs, mspec], [out_shape, mshape]
    out = pl.pallas_call(
        functools.partial(_matmul_kernel, n_parts=len(xs), epilogue=epilogue, with_meta=with_meta),
        grid=(m // bm, n // bn),
        in_specs=in_specs,
        out_specs=out_specs,
        out_shape=out_shape,
        compiler_params=_params(("parallel", "parallel")),
        name="matmul_" + (epilogue or "plain"),
    )(*args)
    return (out[0], out[1][:META_ROWS]) if with_meta else out


def _matmul_kgrid_kernel(*refs, with_meta):
    if with_meta:
        x_ref, w_ref, r_ref, xm_ref, rm_ref, o_ref, om_ref = refs
    else:
        x_ref, w_ref, r_ref, o_ref = refs
    i, k = pl.program_id(0), pl.program_id(2)

    @pl.when(k == 0)
    def _():
        o_ref[...] = r_ref[...]

    o_ref[...] += jnp.dot(x_ref[...], w_ref[...].astype(BF16), preferred_element_type=F32)

    if with_meta:
        @pl.when(k == 0)
        def _():
            om_ref[...] = jnp.where(i == 0, rm_ref[...], 0.0)

        @pl.when(i == 0)
        def _():
            om_ref[...] += jnp.dot(xm_ref[...], w_ref[...].astype(BF16), preferred_element_type=F32)


def matmul_residual_kgrid(x, w_stack, layer, residual, meta_x=None, meta_residual=None):
    m, kdim = x.shape
    n = w_stack.shape[2]
    with_meta = meta_x is not None
    bm = _pick(m, (1024, 512, 256, 128))
    bn = _pick(n, (1024, 512, 256, 128))
    bk = _pick(kdim, (2048, 1024, 512, 256, 128))
    tile = pl.BlockSpec((bm, bn), lambda i, j, k: (i, j))
    in_specs = [
        pl.BlockSpec((bm, bk), lambda i, j, k: (i, k)),
        pl.BlockSpec((None, bk, bn), lambda i, j, k: (layer, k, j)),
        tile,
    ]
    args = [x, w_stack, residual]
    out_specs, out_shape = tile, jax.ShapeDtypeStruct((m, n), F32)
    if with_meta:
        in_specs += [pl.BlockSpec((META_ROWS, bk), lambda i, j, k: (0, k)),
                     pl.BlockSpec((META_ROWS, bn), lambda i, j, k: (0, j))]
        args += [meta_x, meta_residual]
        mspec, mshape = _meta_out(m // bm, n, bn, F32, lambda i, j, k: (i, j))
        out_specs, out_shape = [out_specs, mspec], [out_shape, mshape]
    out = pl.pallas_call(
        functools.partial(_matmul_kgrid_kernel, with_meta=with_meta),
        grid=(m // bm, n // bn, kdim // bk),
        in_specs=in_specs,
        out_specs=out_specs,
        out_shape=out_shape,
        compiler_params=_params(("parallel", "parallel", "arbitrary")),
        name="matmul_residual_kgrid",
    )(*args)
    return (out[0], out[1][:META_ROWS]) if with_meta else out


def _qkv_prep_kernel(q_ref, k_ref, v_ref, gq_ref, gk_ref, pool_ref, qo_ref, ko_ref, vt_ref):
    pool = pool_ref[...]
    width = q_ref.shape[2]
    for h in range(width // HEAD):
        cols = slice(h * HEAD, (h + 1) * HEAD)
        for src, g_ref, dst in ((q_ref, gq_ref, qo_ref), (k_ref, gk_ref, ko_ref)):
            x = src[0, :, cols].astype(F32)
            ms = jnp.dot((x * x).astype(BF16), pool, preferred_element_type=F32)
            dst[0, :, cols] = (x * lax.rsqrt(ms + EPS) * g_ref[...]).astype(dst.dtype)
        vt_ref[0, h, 0] = v_ref[0, :, cols].astype(F32).T.astype(vt_ref.dtype)


def qkv_prep(proj3, q_gain, k_gain, width, tile):
    batch, length, _ = proj3.shape
    n_heads = width // HEAD
    nblk = length // tile
    pool = np.kron(np.eye(2), np.full((HALF, HALF), 1.0 / HALF)).astype(np.float32)
    qk = jax.ShapeDtypeStruct((batch, length, width), BF16)
    vt = jax.ShapeDtypeStruct((batch, n_heads, nblk, HEAD, tile), BF16)
    gspec = pl.BlockSpec((1, HEAD), lambda b, i: (0, 0))
    qkspec = pl.BlockSpec((1, tile, width), lambda b, i: (b, i, 0))
    return pl.pallas_call(
        _qkv_prep_kernel,
        grid=(batch, nblk),
        in_specs=[
            pl.BlockSpec((1, tile, width), lambda b, i: (b, i, 4)),
            pl.BlockSpec((1, tile, width), lambda b, i: (b, i, 5)),
            pl.BlockSpec((1, tile, width), lambda b, i: (b, i, 6)),
            gspec, gspec,
            pl.BlockSpec((HEAD, HEAD), lambda b, i: (0, 0)),
        ],
        out_specs=[qkspec, qkspec,
                   pl.BlockSpec((1, n_heads, 1, HEAD, tile), lambda b, i: (b, 0, i, 0, 0))],
        out_shape=[qk, qk, vt],
        compiler_params=_params(("parallel", "parallel")),
        name="qkv_prep",
    )(proj3, proj3, proj3, q_gain, k_gain, jnp.asarray(pool, BF16))


def _alibi_slopes_log2(n_heads):
    return np.exp2(-8.0 * np.arange(1, n_heads + 1, dtype=np.float64) / n_heads) * LOG2E


def _attn_key_feat(n_heads, rows, mask_pad):
    ki = np.arange(rows)
    feat = (_alibi_slopes_log2(n_heads)[None, :] * ki[:, None]).astype(np.float32)
    if mask_pad:
        feat = np.where((ki < PAD)[:, None], np.float32(MASK_VALUE), feat)
    hi = feat.astype(BF16)
    lo = np.where(feat <= MASK_VALUE, np.float32(0.0), feat - hi.astype(np.float32)).astype(BF16)
    key_feat = np.zeros((rows, n_heads, HEAD), BF16)
    key_feat[..., 0] = hi
    key_feat[..., 1] = lo
    return jnp.asarray(key_feat.reshape(rows, n_heads * HEAD))


def _attn_diag_bias(n_heads, tile):
    slopes = jnp.asarray(_alibi_slopes_log2(n_heads), F32)
    ki = jnp.arange(tile, dtype=jnp.int32)
    kk, qq = ki[:, None], ki[None, :]
    allowed = (kk // ATTN_CHUNK) <= (qq // ATTN_CHUNK)
    rel = (-jnp.abs(qq - kk) + qq - kk).astype(F32)
    return jnp.where(allowed[None], slopes[:, None, None] * rel[None], MASK_VALUE)


def _ones_rows(cols):
    return jnp.broadcast_to((jnp.arange(ONES_ROWS) == 0).astype(BF16)[:, None], (ONES_ROWS, cols))


def _attn_kernel(*refs, tile, heads, with_meta, mask_pad):
    refs = list(refs)
    scal_ref, slope_ref, q_ref, k_ref, vt_ref, kfeat_ref, qfeat_ref, ones_ref, dbias_ref, g_ref = refs[:10]
    if with_meta:
        km_ref, vtm_ref, kfeatm_ref, onesm_ref = refs[10:14]
    o_ref, acc_ref, s_ref = refs[-3:]
    i = pl.program_id(2)
    lam = scal_ref[0]
    out_scale = scal_ref[1]
    qfeat = qfeat_ref[...]
    ones = ones_ref[...]
    lane = lax.broadcasted_iota(jnp.int32, (tile, HEAD), 1)

    def head_cols(h):
        return slice(h * HEAD, (h + 1) * HEAD)

    qcat = []
    for h in range(heads):
        q = q_ref[0, :, head_cols(h)]
        qcat.append(jnp.concatenate(
            [jnp.concatenate([jnp.where(keep, q, jnp.zeros_like(q)), qfeat], axis=1)
             for keep in (lane < HALF, lane >= HALF)], axis=0))
    slope = [slope_ref[pl.program_id(0) * heads + h] for h in range(heads)]
    block_shift = [s * float(tile) for s in slope]

    def scores(kb, feat, h):
        kaug = jnp.concatenate([kb, feat], axis=1)
        return lax.dot_general(kaug, qcat[h], (((1,), (1,)), ((), ())), preferred_element_type=F32)

    def produce(h, j, slot, diag=False):
        kb = k_ref[0, pl.ds(pl.multiple_of(j * tile, tile), tile), head_cols(h)]
        s = scores(kb, kfeat_ref[:, head_cols(h)], h)
        if diag:
            s = s + jnp.concatenate([dbias_ref[h]] * 2, axis=1)
        s_ref[h, slot] = s
        return jnp.max(s, axis=0, keepdims=True)

    def consume(h, j, slot, col_max, m_old):
        vaug = jnp.concatenate([vt_ref[0, h, j], ones], axis=0)
        shift = block_shift[h] * (j - i).astype(F32)
        m_new = jnp.maximum(m_old, col_max + shift)
        p = jnp.exp2(s_ref[h, slot] - (m_new - shift)).astype(BF16)
        acc_ref[h] = acc_ref[h] * jnp.exp2(m_old - m_new) + jnp.dot(vaug, p, preferred_element_type=F32)
        return m_new

    def start(h):
        if not with_meta:
            acc_ref[h] = jnp.zeros(acc_ref.shape[1:], F32)
            return jnp.full((1, 2 * tile), MASK_VALUE, F32)
        s = scores(km_ref[0, :, head_cols(h)], kfeatm_ref[:, head_cols(h)], h)
        shift = -(slope[h] * float(META_ROWS) + block_shift[h] * i.astype(F32))
        m_new = jnp.max(s, axis=0, keepdims=True) + shift
        p = jnp.exp2(s - (m_new - shift)).astype(BF16)
        vaug = jnp.concatenate([vtm_ref[0, h, 0], onesm_ref[...]], axis=0)
        acc_ref[h] = jnp.dot(vaug, p, preferred_element_type=F32)
        return m_new

    init = tuple((start(h), produce(h, i, 0, diag=True)) for h in range(heads))

    def pipe(t, slot, carry):
        prev = jnp.where(t == 0, i, t - 1)
        return tuple((consume(h, prev, slot, carry[h][1], carry[h][0]), produce(h, t, 1 - slot))
                     for h in range(heads))

    carry = lax.fori_loop(0, i // 2, lambda n, c: pipe(2 * n + 1, 1, pipe(2 * n, 0, c)), init)

    def odd_tail(c):
        c = pipe(i - 1, 0, c)
        for h in range(heads):
            consume(h, i - 1, 1, c[h][1], c[h][0])
        return 0

    def even_tail(c):
        for h in range(heads):
            consume(h, jnp.where(i == 0, i, i - 1), 0, c[h][1], c[h][0])
        return 0

    lax.cond(i % 2 == 1, odd_tail, even_tail, carry)

    row = lax.broadcasted_iota(jnp.int32, (tile, HEAD), 0) + i * tile
    for h in range(heads):
        def normalized(c):
            cols = slice(c * tile, (c + 1) * tile)
            return acc_ref[h, :HEAD, cols] / acc_ref[h, HEAD:HEAD + 1, cols]

        o = normalized(0) - lam * normalized(1)
        o = o * lax.rsqrt(jnp.mean(o * o, axis=0, keepdims=True) + EPS)
        ot = o.T * (g_ref[...] * out_scale)
        if mask_pad:
            ot = jnp.where(row >= PAD, ot, 0.0)
        o_ref[0, :, head_cols(h)] = ot.astype(o_ref.dtype)


def diff_attention(qn, kn, vt, lam, lam_init, out_g, tile, meta=None):
    batch, length, width = qn.shape
    n_heads = width // HEAD
    heads = _pick(n_heads, (ATTN_HEADS_PER_STEP, 1))
    nq = length // tile
    with_meta = meta is not None
    scal = jnp.stack([lam.astype(F32), jnp.asarray(1.0 - lam_init, F32)])
    query_feat = jnp.broadcast_to((jnp.arange(HEAD) < 2).astype(BF16)[None, :], (tile, HEAD))
    smem = pl.BlockSpec(memory_space=pltpu.SMEM)
    bw = heads * HEAD
    in_specs = [
        smem, smem,
        pl.BlockSpec((1, tile, bw), lambda g, b, i: (b, i, g)),
        pl.BlockSpec((1, length, bw), lambda g, b, i: (b, 0, g)),
        pl.BlockSpec((1, heads, nq, HEAD, tile), lambda g, b, i: (b, g, 0, 0, 0)),
        pl.BlockSpec((tile, bw), lambda g, b, i: (0, g)),
        pl.BlockSpec((tile, HEAD), lambda g, b, i: (0, 0)),
        pl.BlockSpec((ONES_ROWS, tile), lambda g, b, i: (0, 0)),
        pl.BlockSpec((heads, tile, tile), lambda g, b, i: (g, 0, 0)),
        pl.BlockSpec((1, HEAD), lambda g, b, i: (0, 0)),
    ]
    args = [scal, jnp.asarray(_alibi_slopes_log2(n_heads), F32), qn, kn, vt,
            _attn_key_feat(n_heads, tile, mask_pad=not with_meta), query_feat, _ones_rows(tile),
            _attn_diag_bias(n_heads, tile), out_g.reshape(1, HEAD).astype(F32)]
    if with_meta:
        in_specs += [
            pl.BlockSpec((1, N_META, bw), lambda g, b, i: (0, 0, g)),
            pl.BlockSpec((1, heads, 1, HEAD, N_META), lambda g, b, i: (0, g, 0, 0, 0)),
            pl.BlockSpec((N_META, bw), lambda g, b, i: (0, g)),
            pl.BlockSpec((ONES_ROWS, N_META), lambda g, b, i: (0, 0)),
        ]
        args += [meta[0][:, PAD:], meta[1][..., PAD:],
                 _attn_key_feat(n_heads, META_ROWS, mask_pad=False)[PAD:], _ones_rows(N_META)]
    return pl.pallas_call(
        functools.partial(_attn_kernel, tile=tile, heads=heads, with_meta=with_meta, mask_pad=not with_meta),
        grid=(n_heads // heads, batch, nq),
        in_specs=in_specs,
        out_specs=pl.BlockSpec((1, tile, bw), lambda g, b, i: (b, i, g)),
        out_shape=jax.ShapeDtypeStruct((batch, length, width), BF16),
        scratch_shapes=[pltpu.VMEM((heads, HEAD + ONES_ROWS, 2 * tile), F32),
                        pltpu.VMEM((heads, 2, tile, 2 * tile), F32)],
        compiler_params=_params(("parallel", "parallel", "parallel")),
        name="diff_attention",
    )(*args)


def _hgrn_constants():
    c = HGRN_CHUNK
    t = np.arange(c)[:, None]
    r = np.arange(c)[None, :]
    sums = []
    pair = [np.eye(c)]
    for level in range(1, HGRN_LEVELS + 1):
        size = 2 ** level
        mid = (t // size) * size + size // 2 - 1
        upper = (t % size) >= size // 2
        sums.append(np.where(upper, (r > mid) & (r <= t), (r > t) & (r <= mid)))
        same = (t // size) == (r // size)
        pair.append(same & upper & ((r % size) < size // 2))
    sums.append(r <= t)
    sums.append(r > t)
    sums = np.concatenate(sums, axis=0).astype(np.float32)
    return (jnp.asarray(np.concatenate([sums, sums], axis=1), BF16),
            jnp.asarray(np.stack(pair).astype(np.float32)))


def _hgrn_kernel(*refs, heads, chunks, with_init, emit_state):
    refs = list(refs)
    q_ref, f_ref, v_ref, g_ref, lb_ref, og_ref, sums_ref, pair_ref = refs[:8]
    init_ref = refs[8] if with_init else None
    o_ref = refs[8 + with_init]
    state_out_ref = refs[9 + with_init] if emit_state else None
    state_ref = refs[-1]
    c = HGRN_CHUNK

    @pl.when(pl.program_id(2) == 0)
    def _():
        state_ref[...] = init_ref[...] if with_init else jnp.zeros_like(state_ref)

    def chunk_body(n):
        rows = slice(n * c, (n + 1) * c)
        lb_all = lb_ref[...]
        sig_all = jax.nn.sigmoid(f_ref[0, rows, :].astype(F32))
        logf = jnp.log2(lb_all + (1.0 - lb_all) * sig_all)
        k_all = ((1.0 - lb_all) * (1.0 - sig_all)).astype(BF16)
        hi = logf.astype(BF16)
        lo = (logf - hi.astype(F32)).astype(BF16)
        decay_all = jnp.exp2(jnp.dot(sums_ref[...], jnp.concatenate([hi, lo], axis=0),
                                     preferred_element_type=F32))
        for h in range(heads):
            cols = slice(h * HEAD, (h + 1) * HEAD)
            q = q_ref[0, rows, cols].astype(BF16)
            v = v_ref[0, rows, cols].astype(BF16)
            gate = g_ref[0, rows, cols].astype(F32)
            k = k_all[:, cols]
            decay = decay_all[:, cols]

            a = pair_ref[0] * lax.dot_general(q, k, (((1,), (1,)), ((), ())), preferred_element_type=F32)
            for level in range(1, HGRN_LEVELS + 1):
                d = decay[(level - 1) * c:level * c].astype(BF16)
                a = a + pair_ref[level] * lax.dot_general(q * d, k * d, (((1,), (1,)), ((), ())),
                                                          preferred_element_type=F32)
            from_start = decay[7 * c:8 * c]
            to_end = decay[8 * c:9 * c].astype(BF16)
            state_t = state_ref[h]
            o = jnp.dot(a.astype(BF16), v, preferred_element_type=F32)
            o = o + lax.dot_general(q * from_start.astype(BF16), state_t.astype(BF16),
                                    (((1,), (1,)), ((), ())), preferred_element_type=F32)
            state_ref[h] = state_t * from_start[c - 1:c, :] + lax.dot_general(
                v, k * to_end, (((0,), (0,)), ((), ())), preferred_element_type=F32)

            y = o * lax.rsqrt(jnp.mean(o * o, axis=-1, keepdims=True) + EPS) * og_ref[...]
            o_ref[0, rows, cols] = (y * (gate * jax.nn.sigmoid(gate))).astype(o_ref.dtype)

    for n in range(chunks):
        chunk_body(n)

    if emit_state:
        @pl.when(pl.program_id(2) == pl.num_programs(2) - 1)
        def _():
            state_out_ref[0] = state_ref[...]


def hgrn2(proj3, lb, out_g, width, init_state=None, emit_state=False):
    batch, length, _ = proj3.shape
    n_heads = width // HEAD
    heads = _pick(n_heads, (4, 2, 1))
    tile = _pick(length, (512, 384, 256, 128))
    groups = n_heads // heads
    bw = heads * HEAD
    sums, pair = _hgrn_constants()
    with_init = init_state is not None

    def seg_spec(seg):
        return pl.BlockSpec((1, tile, bw), lambda b, hg, t: (b, t, seg * groups + hg))

    in_specs = [
        seg_spec(0), seg_spec(1), seg_spec(2), seg_spec(3),
        pl.BlockSpec((1, bw), lambda b, hg, t: (0, hg)),
        pl.BlockSpec((1, HEAD), lambda b, hg, t: (0, 0)),
        pl.BlockSpec(sums.shape, lambda b, hg, t: (0, 0)),
        pl.BlockSpec(pair.shape, lambda b, hg, t: (0, 0, 0)),
    ]
    args = [proj3, proj3, proj3, proj3, lb.reshape(1, width).astype(F32),
            out_g.reshape(1, HEAD).astype(F32), sums, pair]
    if with_init:
        in_specs.append(pl.BlockSpec((heads, HEAD, HEAD), lambda b, hg, t: (hg, 0, 0)))
        args.append(init_state)
    out_specs = pl.BlockSpec((1, tile, bw), lambda b, hg, t: (b, t, hg))
    out_shape = jax.ShapeDtypeStruct((batch, length, width), BF16)
    if emit_state:
        out_specs = [out_specs, pl.BlockSpec((1, heads, HEAD, HEAD), lambda b, hg, t: (b, hg, 0, 0))]
        out_shape = [out_shape, jax.ShapeDtypeStruct((batch, n_heads, HEAD, HEAD), F32)]
    return pl.pallas_call(
        functools.partial(_hgrn_kernel, heads=heads, chunks=tile // HGRN_CHUNK, with_init=with_init,
                          emit_state=emit_state),
        grid=(batch, groups, length // tile),
        in_specs=in_specs,
        out_specs=out_specs,
        out_shape=out_shape,
        scratch_shapes=[pltpu.VMEM((heads, HEAD, HEAD), F32)],
        compiler_params=_params(("parallel", "parallel", "arbitrary")),
        name="hgrn2",
    )(*args)


def kernel(x, meta_tokens, norm1_g, w_in, hgrn_lb_raw, hgrn_out_g, q_norm_g, k_norm_g, diff_lambda,
           diff_out_g, w_out, norm2_g, w_mlp_up, w_mlp_down):
    batch, seq, d_model = x.shape
    depth = w_in.shape[0]
    width = d_model // 2
    tile = _pick(seq, (512, 384, 256, 128))
    h = x.reshape(batch * seq, d_model)
    hm = jnp.concatenate([jnp.zeros((PAD, d_model), x.dtype), meta_tokens.astype(x.dtype)], axis=0)

    lb_all = jnp.cumsum(jax.nn.softmax(hgrn_lb_raw.astype(F32), axis=0), axis=0)
    lb_all = lb_all - lb_all[0:1]
    q_scale = LOG2E / math.sqrt(HALF)

    for layer in range(depth):
        keep_meta = layer + 1 < depth
        q_gain = (q_norm_g[layer].reshape(1, HEAD) * q_scale).astype(F32)
        k_gain = k_norm_g[layer].reshape(1, HEAD).astype(F32)
        lp = diff_lambda[layer].astype(F32)
        lam_init = 0.8 - 0.6 * math.exp(-0.3 * layer)
        lam = jnp.exp(jnp.sum(lp[0] * lp[1])) - jnp.exp(jnp.sum(lp[2] * lp[3])) + lam_init

        proj, projm = matmul([rmsnorm(h, norm1_g[layer])], w_in, layer, BF16,
                             meta_xs=[rmsnorm(hm, norm1_g[layer])], single_buffer_rows=True)
        proj3 = proj.reshape(batch, seq, -1)
        projm3 = projm.reshape(1, META_ROWS, -1)
        o_am, state = hgrn2(projm3, lb_all[layer], hgrn_out_g[layer], width, emit_state=True)
        o_a = hgrn2(proj3, lb_all[layer], hgrn_out_g[layer], width, init_state=state[0])
        qn_m, kn_m, vt_m = qkv_prep(projm3, q_gain, k_gain, width, META_ROWS)
        qn, kn, vt = qkv_prep(proj3, q_gain, k_gain, width, tile)
        o_b = diff_attention(qn, kn, vt, lam, lam_init, diff_out_g[layer], tile, meta=(kn_m, vt_m))
        mixed = [o_a.reshape(batch * seq, width), o_b.reshape(batch * seq, width)]
        if keep_meta:
            o_bm = diff_attention(qn_m, kn_m, vt_m, lam, lam_init, diff_out_g[layer], META_ROWS)
            h, hm = matmul(mixed, w_out, layer, F32, epilogue="residual", residual=h,
                           meta_xs=[o_am.reshape(META_ROWS, width), o_bm.reshape(META_ROWS, width)],
                           meta_residual=hm)
            z, zm = matmul([rmsnorm(h, norm2_g[layer])], w_mlp_up, layer, BF16, epilogue="relu2",
                           meta_xs=[rmsnorm(hm, norm2_g[layer])], single_buffer_rows=True)
            h, hm = matmul_residual_kgrid(z, w_mlp_down, layer, h, meta_x=zm, meta_residual=hm)
        else:
            h = matmul(mixed, w_out, layer, F32, epilogue="residual", residual=h)
            z = matmul([rmsnorm(h, norm2_g[layer])], w_mlp_up, layer, BF16, epilogue="relu2",
                       single_buffer_rows=True)
            h = matmul_residual_kgrid(z, w_mlp_down, layer, h)
    return h.reshape(batch, seq, d_model)
```

```python
import functools
import math

import numpy as np
import jax
import jax.numpy as jnp
from jax import lax
from jax.experimental import pallas as pl
from jax.experimental.pallas import tpu as pltpu

N_META = 16
ATTN_CHUNK = 64
HEAD = 128
HALF = HEAD // 2
META_ROWS = 128
PAD = META_ROWS - N_META
HGRN_CHUNK = 128
HGRN_LEVELS = 7
ONES_ROWS = 16
ATTN_HEADS_PER_STEP = 4
EPS = 1e-6
MASK_VALUE = -1e30
LOG2E = math.log2(math.e)
V7X_VMEM_LIMIT_BYTES = 56 * 1024 * 1024
DENSE_WINDOW_BUDGET_BYTES = 50 * 1024 * 1024

F32 = jnp.float32
BF16 = jnp.bfloat16


def _params(semantics):
    return pltpu.CompilerParams(dimension_semantics=semantics, vmem_limit_bytes=V7X_VMEM_LIMIT_BYTES)


def _pick(n, candidates):
    for c in candidates:
        if n % c == 0:
            return c
    raise ValueError(f"no tile for {n} among {candidates}")


def _rmsnorm_kernel(x_ref, g_ref, o_ref):
    x = x_ref[...]
    ms = jnp.mean(x * x, axis=-1, keepdims=True)
    o_ref[...] = (x * lax.rsqrt(ms + EPS) * g_ref[...]).astype(o_ref.dtype)


def rmsnorm(x, g):
    m, d = x.shape
    bm = _pick(m, (512, 256, 128))
    return pl.pallas_call(
        _rmsnorm_kernel,
        grid=(m // bm,),
        in_specs=[pl.BlockSpec((bm, d), lambda i: (i, 0)), pl.BlockSpec((1, d), lambda i: (0, 0))],
        out_specs=pl.BlockSpec((bm, d), lambda i: (i, 0)),
        out_shape=jax.ShapeDtypeStruct((m, d), BF16),
        compiler_params=_params(("parallel",)),
        name="rmsnorm",
    )(x, g.reshape(1, d).astype(F32))


def _epilogue(acc, epilogue, r_ref):
    if epilogue == "relu2":
        acc = jnp.square(jnp.maximum(acc, 0.0))
    if epilogue == "residual":
        acc = acc + r_ref[...]
    return acc


def _matmul_kernel(*refs, n_parts, epilogue, with_meta):
    refs = list(refs)
    take = lambda n: [refs.pop(0) for _ in range(n)]
    xs, ws = take(n_parts), take(n_parts)
    r_ref = refs.pop(0) if epilogue == "residual" else None
    xms = take(n_parts) if with_meta else []
    rm_ref = refs.pop(0) if with_meta and epilogue == "residual" else None
    o_ref = refs.pop(0)

    def product(x_refs, res_ref):
        acc = jnp.dot(x_refs[0][...], ws[0][...].astype(BF16), preferred_element_type=F32)
        for x_ref, w_ref in zip(x_refs[1:], ws[1:]):
            acc = acc + jnp.dot(x_ref[...], w_ref[...].astype(BF16), preferred_element_type=F32)
        return _epilogue(acc, epilogue, res_ref)

    o_ref[...] = product(xs, r_ref).astype(o_ref.dtype)
    if with_meta:
        om_ref = refs.pop(0)

        @pl.when(pl.program_id(0) == 0)
        def _():
            om_ref[...] = product(xms, rm_ref).astype(om_ref.dtype)

        @pl.when(pl.program_id(0) != 0)
        def _():
            om_ref[...] = jnp.zeros_like(om_ref)


def _meta_out(row_tiles, n, bn, dtype, index_map):
    return (pl.BlockSpec((META_ROWS, bn), index_map),
            jax.ShapeDtypeStruct((row_tiles * META_ROWS, n), dtype))


def matmul(xs, w_stack, layer, out_dtype, epilogue=None, residual=None, meta_xs=None, meta_residual=None,
           single_buffer_rows=False):
    m = xs[0].shape[0]
    n = w_stack.shape[2]
    d_in = sum(x.shape[1] for x in xs)
    with_meta = meta_xs is not None
    bn = _pick(n, (512, 256, 128))

    row_buffers = 1 if single_buffer_rows else 2

    def vmem_bytes(bm):
        per_step = d_in * bn * 4 + bm * bn * jnp.dtype(out_dtype).itemsize
        if epilogue == "residual":
            per_step += bm * bn * 4
        return 2 * per_step + row_buffers * bm * d_in * 2

    bm = next(c for c in (2048, 1024, 512, 256, 128)
              if m % c == 0 and vmem_bytes(c) <= DENSE_WINDOW_BUDGET_BYTES)
    row_mode = dict(pipeline_mode=pl.Buffered(1)) if single_buffer_rows else {}
    in_specs = [pl.BlockSpec((bm, x.shape[1]), lambda i, j: (i, 0), **row_mode) for x in xs]
    row_block = 0
    for x in xs:
        kp = x.shape[1]
        in_specs.append(pl.BlockSpec((None, kp, bn), functools.partial(
            lambda i, j, rb: (layer, rb, j), rb=row_block // kp)))
        row_block += kp
    args = list(xs) + [w_stack] * len(xs)
    if epilogue == "residual":
        in_specs.append(pl.BlockSpec((bm, bn), lambda i, j: (i, j)))
        args.append(residual)
    out_specs = pl.BlockSpec((bm, bn), lambda i, j: (i, j))
    out_shape = jax.ShapeDtypeStruct((m, n), out_dtype)
    if with_meta:
        in_specs += [pl.BlockSpec((META_ROWS, x.shape[1]), lambda i, j: (0, 0)) for x in meta_xs]
        args += list(meta_xs)
        if epilogue == "residual":
            in_specs.append(pl.BlockSpec((META_ROWS, bn), lambda i, j: (0, j)))
            args.append(meta_residual)
        mspec, mshape = _meta_out(m // bm, n, bn, out_dtype, lambda i, j: (i, j))
        out_specs, out_shape = [out_specs, mspec], [out_shape, mshape]
    out = pl.pallas_call(
        functools.partial(_matmul_kernel, n_parts=len(xs), epilogue=epilogue, with_meta=with_meta),
        grid=(m // bm, n // bn),
        in_specs=in_specs,
        out_specs=out_specs,
        out_shape=out_shape,
        compiler_params=_params(("parallel", "parallel")),
        name="matmul_" + (epilogue or "plain"),
    )(*args)
    return (out[0], out[1][:META_ROWS]) if with_meta else out


def _matmul_kgrid_kernel(*refs, with_meta):
    if with_meta:
        x_ref, w_ref, r_ref, xm_ref, rm_ref, o_ref, om_ref = refs
    else:
        x_ref, w_ref, r_ref, o_ref = refs
    i, k = pl.program_id(0), pl.program_id(2)

    @pl.when(k == 0)
    def _():
        o_ref[...] = r_ref[...]
        if with_meta:
            om_ref[...] = jnp.where(i == 0, rm_ref[...], 0.0)

    o_ref[...] += jnp.dot(x_ref[...], w_ref[...].astype(BF16), preferred_element_type=F32)

    if with_meta:
        @pl.when(i == 0)
        def _():
            om_ref[...] += jnp.dot(xm_ref[...], w_ref[...].astype(BF16), preferred_element_type=F32)


def matmul_residual_kgrid(x, w_stack, layer, residual, meta_x=None, meta_residual=None):
    m, kdim = x.shape
    n = w_stack.shape[2]
    with_meta = meta_x is not None
    bm = _pick(m, (1024, 512, 256, 128))
    bn = _pick(n, (1024, 512, 256, 128))
    bk = _pick(kdim, (2048, 1024, 512, 256, 128))
    tile = pl.BlockSpec((bm, bn), lambda i, j, k: (i, j))
    in_specs = [
        pl.BlockSpec((bm, bk), lambda i, j, k: (i, k)),
        pl.BlockSpec((None, bk, bn), lambda i, j, k: (layer, k, j)),
        tile,
    ]
    args = [x, w_stack, residual]
    out_specs, out_shape = tile, jax.ShapeDtypeStruct((m, n), F32)
    if with_meta:
        in_specs += [pl.BlockSpec((META_ROWS, bk), lambda i, j, k: (0, k)),
                     pl.BlockSpec((META_ROWS, bn), lambda i, j, k: (0, j))]
        args += [meta_x, meta_residual]
        mspec, mshape = _meta_out(m // bm, n, bn, F32, lambda i, j, k: (i, j))
        out_specs, out_shape = [out_specs, mspec], [out_shape, mshape]
    out = pl.pallas_call(
        functools.partial(_matmul_kgrid_kernel, with_meta=with_meta),
        grid=(m // bm, n // bn, kdim // bk),
        in_specs=in_specs,
        out_specs=out_specs,
        out_shape=out_shape,
        compiler_params=_params(("parallel", "parallel", "arbitrary")),
        name="matmul_residual_kgrid",
    )(*args)
    return (out[0], out[1][:META_ROWS]) if with_meta else out


def _qkv_prep_kernel(q_ref, k_ref, v_ref, gq_ref, gk_ref, pool_ref, qo_ref, ko_ref, vt_ref):
    pool = pool_ref[...]
    width = q_ref.shape[2]
    for h in range(width // HEAD):
        cols = slice(h * HEAD, (h + 1) * HEAD)
        for src, g_ref, dst in ((q_ref, gq_ref, qo_ref), (k_ref, gk_ref, ko_ref)):
            x = src[0, :, cols].astype(F32)
            ms = jnp.dot((x * x).astype(BF16), pool, preferred_element_type=F32)
            dst[0, :, cols] = (x * lax.rsqrt(ms + EPS) * g_ref[...]).astype(dst.dtype)
        vt_ref[0, h, 0] = v_ref[0, :, cols].astype(F32).T.astype(vt_ref.dtype)


def qkv_prep(proj3, q_gain, k_gain, width, tile):
    batch, length, _ = proj3.shape
    n_heads = width // HEAD
    nblk = length // tile
    pool = np.kron(np.eye(2), np.full((HALF, HALF), 1.0 / HALF)).astype(np.float32)
    qk = jax.ShapeDtypeStruct((batch, length, width), BF16)
    vt = jax.ShapeDtypeStruct((batch, n_heads, nblk, HEAD, tile), BF16)
    gspec = pl.BlockSpec((1, HEAD), lambda b, i: (0, 0))
    qkspec = pl.BlockSpec((1, tile, width), lambda b, i: (b, i, 0))
    return pl.pallas_call(
        _qkv_prep_kernel,
        grid=(batch, nblk),
        in_specs=[
            pl.BlockSpec((1, tile, width), lambda b, i: (b, i, 4)),
            pl.BlockSpec((1, tile, width), lambda b, i: (b, i, 5)),
            pl.BlockSpec((1, tile, width), lambda b, i: (b, i, 6)),
            gspec, gspec,
            pl.BlockSpec((HEAD, HEAD), lambda b, i: (0, 0)),
        ],
        out_specs=[qkspec, qkspec,
                   pl.BlockSpec((1, n_heads, 1, HEAD, tile), lambda b, i: (b, 0, i, 0, 0))],
        out_shape=[qk, qk, vt],
        compiler_params=_params(("parallel", "parallel")),
        name="qkv_prep",
    )(proj3, proj3, proj3, q_gain, k_gain, jnp.asarray(pool, BF16))


def _alibi_slopes_log2(n_heads):
    return np.exp2(-8.0 * np.arange(1, n_heads + 1, dtype=np.float64) / n_heads) * LOG2E


def _attn_key_feat(n_heads, rows, mask_pad):
    ki = np.arange(rows)
    feat = (_alibi_slopes_log2(n_heads)[None, :] * ki[:, None]).astype(np.float32)
    if mask_pad:
        feat = np.where((ki < PAD)[:, None], np.float32(MASK_VALUE), feat)
    hi = feat.astype(BF16)
    lo = np.where(feat <= MASK_VALUE, np.float32(0.0), feat - hi.astype(np.float32)).astype(BF16)
    key_feat = np.zeros((rows, n_heads, HEAD), BF16)
    key_feat[..., 0] = hi
    key_feat[..., 1] = lo
    return jnp.asarray(key_feat.reshape(rows, n_heads * HEAD))


def _attn_diag_bias(n_heads, tile):
    slopes = jnp.asarray(_alibi_slopes_log2(n_heads), F32)
    ki = jnp.arange(tile, dtype=jnp.int32)
    kk, qq = ki[:, None], ki[None, :]
    allowed = (kk // ATTN_CHUNK) <= (qq // ATTN_CHUNK)
    rel = (-jnp.abs(qq - kk) + qq - kk).astype(F32)
    return jnp.where(allowed[None], slopes[:, None, None] * rel[None], MASK_VALUE)


def _ones_rows(cols):
    return jnp.broadcast_to((jnp.arange(ONES_ROWS) == 0).astype(BF16)[:, None], (ONES_ROWS, cols))


def _attn_kernel(*refs, tile, heads, with_meta, mask_pad):
    refs = list(refs)
    scal_ref, slope_ref, q_ref, k_ref, vt_ref, kfeat_ref, qfeat_ref, ones_ref, dbias_ref, g_ref = refs[:10]
    if with_meta:
        km_ref, vtm_ref, kfeatm_ref, onesm_ref = refs[10:14]
    o_ref, acc_ref, s_ref = refs[-3:]
    i = pl.program_id(2)
    lam = scal_ref[0]
    out_scale = scal_ref[1]
    qfeat = qfeat_ref[...]
    ones = ones_ref[...]
    lane = lax.broadcasted_iota(jnp.int32, (tile, HEAD), 1)

    def head_cols(h):
        return slice(h * HEAD, (h + 1) * HEAD)

    qcat = []
    for h in range(heads):
        q = q_ref[0, :, head_cols(h)]
        qcat.append(jnp.concatenate(
            [jnp.concatenate([jnp.where(keep, q, jnp.zeros_like(q)), qfeat], axis=1)
             for keep in (lane < HALF, lane >= HALF)], axis=0))
    slope = [slope_ref[pl.program_id(0) * heads + h] for h in range(heads)]
    block_shift = [s * float(tile) for s in slope]

    def scores(kb, feat, h):
        kaug = jnp.concatenate([kb, feat], axis=1)
        return lax.dot_general(kaug, qcat[h], (((1,), (1,)), ((), ())), preferred_element_type=F32)

    def produce(h, j, slot, diag=False):
        kb = k_ref[0, pl.ds(pl.multiple_of(j * tile, tile), tile), head_cols(h)]
        s = scores(kb, kfeat_ref[:, head_cols(h)], h)
        if diag:
            s = s + jnp.concatenate([dbias_ref[h]] * 2, axis=1)
        s_ref[h, slot] = s
        return jnp.max(s, axis=0, keepdims=True)

    def consume(h, j, slot, col_max, m_old):
        vaug = jnp.concatenate([vt_ref[0, h, j], ones], axis=0)
        shift = block_shift[h] * (j - i).astype(F32)
        m_new = jnp.maximum(m_old, col_max + shift)
        p = jnp.exp2(s_ref[h, slot] - (m_new - shift)).astype(BF16)
        acc_ref[h] = acc_ref[h] * jnp.exp2(m_old - m_new) + jnp.dot(vaug, p, preferred_element_type=F32)
        return m_new

    def start(h):
        if not with_meta:
            acc_ref[h] = jnp.zeros(acc_ref.shape[1:], F32)
            return jnp.full((1, 2 * tile), MASK_VALUE, F32)
        s = scores(km_ref[0, :, head_cols(h)], kfeatm_ref[:, head_cols(h)], h)
        shift = -(slope[h] * float(META_ROWS) + block_shift[h] * i.astype(F32))
        m_new = jnp.max(s, axis=0, keepdims=True) + shift
        p = jnp.exp2(s - (m_new - shift)).astype(BF16)
        vaug = jnp.concatenate([vtm_ref[0, h, 0], onesm_ref[...]], axis=0)
        acc_ref[h] = jnp.dot(vaug, p, preferred_element_type=F32)
        return m_new

    init = tuple((start(h), produce(h, i, 0, diag=True)) for h in range(heads))

    def pipe(t, slot, carry):
        prev = jnp.where(t == 0, i, t - 1)
        return tuple((consume(h, prev, slot, carry[h][1], carry[h][0]), produce(h, t, 1 - slot))
                     for h in range(heads))

    carry = lax.fori_loop(0, i // 2, lambda n, c: pipe(2 * n + 1, 1, pipe(2 * n, 0, c)), init)

    def odd_tail(c):
        c = pipe(i - 1, 0, c)
        for h in range(heads):
            consume(h, i - 1, 1, c[h][1], c[h][0])
        return 0

    def even_tail(c):
        for h in range(heads):
            consume(h, jnp.where(i == 0, i, i - 1), 0, c[h][1], c[h][0])
        return 0

    lax.cond(i % 2 == 1, odd_tail, even_tail, carry)

    row = lax.broadcasted_iota(jnp.int32, (tile, HEAD), 0) + i * tile
    for h in range(heads):
        def normalized(c):
            cols = slice(c * tile, (c + 1) * tile)
            return acc_ref[h, :HEAD, cols] / acc_ref[h, HEAD:HEAD + 1, cols]

        o = normalized(0) - lam * normalized(1)
        o = o * lax.rsqrt(jnp.mean(o * o, axis=0, keepdims=True) + EPS)
        ot = o.T * (g_ref[...] * out_scale)
        if mask_pad:
            ot = jnp.where(row >= PAD, ot, 0.0)
        o_ref[0, :, head_cols(h)] = ot.astype(o_ref.dtype)


def diff_attention(qn, kn, vt, lam, lam_init, out_g, tile, meta=None):
    batch, length, width = qn.shape
    n_heads = width // HEAD
    heads = _pick(n_heads, (ATTN_HEADS_PER_STEP, 1))
    nq = length // tile
    with_meta = meta is not None
    scal = jnp.stack([lam.astype(F32), jnp.asarray(1.0 - lam_init, F32)])
    query_feat = jnp.broadcast_to((jnp.arange(HEAD) < 2).astype(BF16)[None, :], (tile, HEAD))
    smem = pl.BlockSpec(memory_space=pltpu.SMEM)
    bw = heads * HEAD
    in_specs = [
        smem, smem,
        pl.BlockSpec((1, tile, bw), lambda g, b, i: (b, i, g)),
        pl.BlockSpec((1, length, bw), lambda g, b, i: (b, 0, g)),
        pl.BlockSpec((1, heads, nq, HEAD, tile), lambda g, b, i: (b, g, 0, 0, 0)),
        pl.BlockSpec((tile, bw), lambda g, b, i: (0, g)),
        pl.BlockSpec((tile, HEAD), lambda g, b, i: (0, 0)),
        pl.BlockSpec((ONES_ROWS, tile), lambda g, b, i: (0, 0)),
        pl.BlockSpec((heads, tile, tile), lambda g, b, i: (g, 0, 0)),
        pl.BlockSpec((1, HEAD), lambda g, b, i: (0, 0)),
    ]
    args = [scal, jnp.asarray(_alibi_slopes_log2(n_heads), F32), qn, kn, vt,
            _attn_key_feat(n_heads, tile, mask_pad=not with_meta), query_feat, _ones_rows(tile),
            _attn_diag_bias(n_heads, tile), out_g.reshape(1, HEAD).astype(F32)]
    if with_meta:
        in_specs += [
            pl.BlockSpec((1, N_META, bw), lambda g, b, i: (0, 0, g)),
            pl.BlockSpec((1, heads, 1, HEAD, N_META), lambda g, b, i: (0, g, 0, 0, 0)),
            pl.BlockSpec((N_META, bw), lambda g, b, i: (0, g)),
            pl.BlockSpec((ONES_ROWS, N_META), lambda g, b, i: (0, 0)),
        ]
        args += [meta[0][:, PAD:], meta[1][..., PAD:],
                 _attn_key_feat(n_heads, META_ROWS, mask_pad=False)[PAD:], _ones_rows(N_META)]
    return pl.pallas_call(
        functools.partial(_attn_kernel, tile=tile, heads=heads, with_meta=with_meta, mask_pad=not with_meta),
        grid=(n_heads // heads, batch, nq),
        in_specs=in_specs,
        out_specs=pl.BlockSpec((1, tile, bw), lambda g, b, i: (b, i, g)),
        out_shape=jax.ShapeDtypeStruct((batch, length, width), BF16),
        scratch_shapes=[pltpu.VMEM((heads, HEAD + ONES_ROWS, 2 * tile), F32),
                        pltpu.VMEM((heads, 2, tile, 2 * tile), F32)],
        compiler_params=_params(("parallel", "parallel", "parallel")),
        name="diff_attention",
    )(*args)


def _hgrn_constants():
    c = HGRN_CHUNK
    t = np.arange(c)[:, None]
    r = np.arange(c)[None, :]
    sums = []
    pair = [np.eye(c)]
    for level in range(1, HGRN_LEVELS + 1):
        size = 2 ** level
        mid = (t // size) * size + size // 2 - 1
        upper = (t % size) >= size // 2
        sums.append(np.where(upper, (r > mid) & (r <= t), (r > t) & (r <= mid)))
        same = (t // size) == (r // size)
        pair.append(same & upper & ((r % size) < size // 2))
    sums.append(r <= t)
    sums.append(r > t)
    sums = np.concatenate(sums, axis=0).astype(np.float32)
    return (jnp.asarray(np.concatenate([sums, sums], axis=1), BF16),
            jnp.asarray(np.stack(pair).astype(np.float32)))


def _hgrn_kernel(*refs, heads, chunks, with_init, emit_state):
    refs = list(refs)
    q_ref, f_ref, v_ref, g_ref, lb_ref, og_ref, sums_ref, pair_ref = refs[:8]
    init_ref = refs[8] if with_init else None
    o_ref = refs[8 + with_init]
    state_out_ref = refs[9 + with_init] if emit_state else None
    state_ref = refs[-1]
    c = HGRN_CHUNK

    @pl.when(pl.program_id(2) == 0)
    def _():
        state_ref[...] = init_ref[...] if with_init else jnp.zeros_like(state_ref)

    def chunk_body(n):
        rows = slice(n * c, (n + 1) * c)
        lb_all = lb_ref[...]
        sig_all = jax.nn.sigmoid(f_ref[0, rows, :].astype(F32))
        logf = jnp.log2(lb_all + (1.0 - lb_all) * sig_all)
        k_all = ((1.0 - lb_all) * (1.0 - sig_all)).astype(BF16)
        hi = logf.astype(BF16)
        lo = (logf - hi.astype(F32)).astype(BF16)
        decay_all = jnp.exp2(jnp.dot(sums_ref[...], jnp.concatenate([hi, lo], axis=0),
                                     preferred_element_type=F32))
        for h in range(heads):
            cols = slice(h * HEAD, (h + 1) * HEAD)
            q = q_ref[0, rows, cols].astype(BF16)
            v = v_ref[0, rows, cols].astype(BF16)
            gate = g_ref[0, rows, cols].astype(F32)
            k = k_all[:, cols]
            decay = decay_all[:, cols]

            a = pair_ref[0] * lax.dot_general(q, k, (((1,), (1,)), ((), ())), preferred_element_type=F32)
            for level in range(1, HGRN_LEVELS + 1):
                d = decay[(level - 1) * c:level * c].astype(BF16)
                a = a + pair_ref[level] * lax.dot_general(q * d, k * d, (((1,), (1,)), ((), ())),
                                                          preferred_element_type=F32)
            from_start = decay[7 * c:8 * c]
            to_end = decay[8 * c:9 * c].astype(BF16)
            state_t = state_ref[h]
            o = jnp.dot(a.astype(BF16), v, preferred_element_type=F32)
            o = o + lax.dot_general(q * from_start.astype(BF16), state_t.astype(BF16),
                                    (((1,), (1,)), ((), ())), preferred_element_type=F32)
            state_ref[h] = state_t * from_start[c - 1:c, :] + lax.dot_general(
                v, k * to_end, (((0,), (0,)), ((), ())), preferred_element_type=F32)

            y = o * lax.rsqrt(jnp.mean(o * o, axis=-1, keepdims=True) + EPS) * og_ref[...]
            o_ref[0, rows, cols] = (y * (gate * jax.nn.sigmoid(gate))).astype(o_ref.dtype)

    for n in range(chunks):
        chunk_body(n)

    if emit_state:
        @pl.when(pl.program_id(2) == pl.num_programs(2) - 1)
        def _():
            state_out_ref[0] = state_ref[...]


def hgrn2(proj3, lb, out_g, width, init_state=None, emit_state=False):
    batch, length, _ = proj3.shape
    n_heads = width // HEAD
    heads = _pick(n_heads, (4, 2, 1))
    tile = _pick(length, (1024, 512, 384, 256, 128))
    groups = n_heads // heads
    bw = heads * HEAD
    sums, pair = _hgrn_constants()
    with_init = init_state is not None

    def seg_spec(seg):
        return pl.BlockSpec((1, tile, bw), lambda b, hg, t: (b, t, seg * groups + hg))

    in_specs = [
        seg_spec(0), seg_spec(1), seg_spec(2), seg_spec(3),
        pl.BlockSpec((1, bw), lambda b, hg, t: (0, hg)),
        pl.BlockSpec((1, HEAD), lambda b, hg, t: (0, 0)),
        pl.BlockSpec(sums.shape, lambda b, hg, t: (0, 0)),
        pl.BlockSpec(pair.shape, lambda b, hg, t: (0, 0, 0)),
    ]
    args = [proj3, proj3, proj3, proj3, lb.reshape(1, width).astype(F32),
            out_g.reshape(1, HEAD).astype(F32), sums, pair]
    if with_init:
        in_specs.append(pl.BlockSpec((heads, HEAD, HEAD), lambda b, hg, t: (hg, 0, 0)))
        args.append(init_state)
    out_specs = pl.BlockSpec((1, tile, bw), lambda b, hg, t: (b, t, hg))
    out_shape = jax.ShapeDtypeStruct((batch, length, width), BF16)
    if emit_state:
        out_specs = [out_specs, pl.BlockSpec((1, heads, HEAD, HEAD), lambda b, hg, t: (b, hg, 0, 0))]
        out_shape = [out_shape, jax.ShapeDtypeStruct((batch, n_heads, HEAD, HEAD), F32)]
    return pl.pallas_call(
        functools.partial(_hgrn_kernel, heads=heads, chunks=tile // HGRN_CHUNK, with_init=with_init,
                          emit_state=emit_state),
        grid=(batch, groups, length // tile),
        in_specs=in_specs,
        out_specs=out_specs,
        out_shape=out_shape,
        scratch_shapes=[pltpu.VMEM((heads, HEAD, HEAD), F32)],
        compiler_params=_params(("parallel", "parallel", "arbitrary")),
        name="hgrn2",
    )(*args)


def kernel(x, meta_tokens, norm1_g, w_in, hgrn_lb_raw, hgrn_out_g, q_norm_g, k_norm_g, diff_lambda,
           diff_out_g, w_out, norm2_g, w_mlp_up, w_mlp_down):
    batch, seq, d_model = x.shape
    depth = w_in.shape[0]
    width = d_model // 2
    tile = _pick(seq, (512, 384, 256, 128))
    h = x.reshape(batch * seq, d_model)
    hm = jnp.concatenate([jnp.zeros((PAD, d_model), x.dtype), meta_tokens.astype(x.dtype)], axis=0)

    lb_all = jnp.cumsum(jax.nn.softmax(hgrn_lb_raw.astype(F32), axis=0), axis=0)
    lb_all = lb_all - lb_all[0:1]
    q_scale = LOG2E / math.sqrt(HALF)

    for layer in range(depth):
        keep_meta = layer + 1 < depth
        q_gain = (q_norm_g[layer].reshape(1, HEAD) * q_scale).astype(F32)
        k_gain = k_norm_g[layer].reshape(1, HEAD).astype(F32)
        lp = diff_lambda[layer].astype(F32)
        lam_init = 0.8 - 0.6 * math.exp(-0.3 * layer)
        lam = jnp.exp(jnp.sum(lp[0] * lp[1])) - jnp.exp(jnp.sum(lp[2] * lp[3])) + lam_init

        proj, projm = matmul([rmsnorm(h, norm1_g[layer])], w_in, layer, BF16,
                             meta_xs=[rmsnorm(hm, norm1_g[layer])], single_buffer_rows=True)
        proj3 = proj.reshape(batch, seq, -1)
        projm3 = projm.reshape(1, META_ROWS, -1)
        o_am, state = hgrn2(projm3, lb_all[layer], hgrn_out_g[layer], width, emit_state=True)
        o_a = hgrn2(proj3, lb_all[layer], hgrn_out_g[layer], width, init_state=state[0])
        qn_m, kn_m, vt_m = qkv_prep(projm3, q_gain, k_gain, width, META_ROWS)
        qn, kn, vt = qkv_prep(proj3, q_gain, k_gain, width, tile)
        o_b = diff_attention(qn, kn, vt, lam, lam_init, diff_out_g[layer], tile, meta=(kn_m, vt_m))
        mixed = [o_a.reshape(batch * seq, width), o_b.reshape(batch * seq, width)]
        if keep_meta:
            o_bm = diff_attention(qn_m, kn_m, vt_m, lam, lam_init, diff_out_g[layer], META_ROWS)
            h, hm = matmul(mixed, w_out, layer, F32, epilogue="residual", residual=h,
                           meta_xs=[o_am.reshape(META_ROWS, width), o_bm.reshape(META_ROWS, width)],
                           meta_residual=hm)
            z, zm = matmul([rmsnorm(h, norm2_g[layer])], w_mlp_up, layer, BF16, epilogue="relu2",
                           meta_xs=[rmsnorm(hm, norm2_g[layer])], single_buffer_rows=True)
            h, hm = matmul_residual_kgrid(z, w_mlp_down, layer, h, meta_x=zm, meta_residual=hm)
        else:
            h = matmul(mixed, w_out, layer, F32, epilogue="residual", residual=h)
            z = matmul([rmsnorm(h, norm2_g[layer])], w_mlp_up, layer, BF16, epilogue="relu2",
                       single_buffer_rows=True)
            h = matmul_residual_kgrid(z, w_mlp_down, layer, h)
    return h.reshape(batch, seq, d_model)
```

```python
import functools
import math

import numpy as np
import jax
import jax.numpy as jnp
from jax import lax
from jax.experimental import pallas as pl
from jax.experimental.pallas import tpu as pltpu

N_META = 16
ATTN_CHUNK = 64
HEAD = 128
HALF = HEAD // 2
META_ROWS = 128
PAD = META_ROWS - N_META
HGRN_CHUNK = 128
HGRN_LEVELS = 7
ONES_ROWS = 16
ATTN_HEADS_PER_STEP = 4
EPS = 1e-6
MASK_VALUE = -1e30
LOG2E = math.log2(math.e)
V7X_VMEM_LIMIT_BYTES = 56 * 1024 * 1024
DENSE_WINDOW_BUDGET_BYTES = 50 * 1024 * 1024

F32 = jnp.float32
BF16 = jnp.bfloat16


def _params(semantics):
    return pltpu.CompilerParams(dimension_semantics=semantics, vmem_limit_bytes=V7X_VMEM_LIMIT_BYTES)


def _pick(n, candidates):
    for c in candidates:
        if n % c == 0:
            return c
    raise ValueError(f"no tile for {n} among {candidates}")


def _rmsnorm_kernel(x_ref, g_ref, o_ref):
    x = x_ref[...]
    ms = jnp.mean(x * x, axis=-1, keepdims=True)
    o_ref[...] = (x * lax.rsqrt(ms + EPS) * g_ref[...]).astype(o_ref.dtype)


def rmsnorm(x, g):
    m, d = x.shape
    bm = _pick(m, (512, 256, 128))
    return pl.pallas_call(
        _rmsnorm_kernel,
        grid=(m // bm,),
        in_specs=[pl.BlockSpec((bm, d), lambda i: (i, 0)), pl.BlockSpec((1, d), lambda i: (0, 0))],
        out_specs=pl.BlockSpec((bm, d), lambda i: (i, 0)),
        out_shape=jax.ShapeDtypeStruct((m, d), BF16),
        compiler_params=_params(("parallel",)),
        name="rmsnorm",
    )(x, g.reshape(1, d).astype(F32))


def _epilogue(acc, epilogue, r_ref):
    if epilogue == "relu2":
        acc = jnp.square(jnp.maximum(acc, 0.0))
    if epilogue == "residual":
        acc = acc + r_ref[...]
    return acc


def _matmul_kernel(*refs, n_parts, epilogue, with_meta):
    refs = list(refs)
    take = lambda n: [refs.pop(0) for _ in range(n)]
    xs, ws = take(n_parts), take(n_parts)
    r_ref = refs.pop(0) if epilogue == "residual" else None
    xms = take(n_parts) if with_meta else []
    rm_ref = refs.pop(0) if with_meta and epilogue == "residual" else None
    o_ref = refs.pop(0)

    def product(x_refs, res_ref):
        acc = jnp.dot(x_refs[0][...], ws[0][...].astype(BF16), preferred_element_type=F32)
        for x_ref, w_ref in zip(x_refs[1:], ws[1:]):
            acc = acc + jnp.dot(x_ref[...], w_ref[...].astype(BF16), preferred_element_type=F32)
        return _epilogue(acc, epilogue, res_ref)

    o_ref[...] = product(xs, r_ref).astype(o_ref.dtype)
    if with_meta:
        om_ref = refs.pop(0)

        @pl.when(pl.program_id(0) == 0)
        def _():
            om_ref[...] = product(xms, rm_ref).astype(om_ref.dtype)

        @pl.when(pl.program_id(0) != 0)
        def _():
            om_ref[...] = jnp.zeros_like(om_ref)


def _meta_col(i, j):
    return jnp.where(i == 0, j, 0)


def _meta_out(n, bn, dtype):
    return (lambda index_map: pl.BlockSpec((META_ROWS, bn), index_map),
            jax.ShapeDtypeStruct((2 * META_ROWS, n), dtype))


def matmul(xs, w_stack, layer, out_dtype, epilogue=None, residual=None, meta_xs=None, meta_residual=None,
           single_buffer_rows=False):
    m = xs[0].shape[0]
    n = w_stack.shape[2]
    d_in = sum(x.shape[1] for x in xs)
    with_meta = meta_xs is not None
    bn = _pick(n, (512, 256, 128))

    row_buffers = 1 if single_buffer_rows else 2

    def vmem_bytes(bm):
        per_step = d_in * bn * 4 + bm * bn * jnp.dtype(out_dtype).itemsize
        if epilogue == "residual":
            per_step += bm * bn * 4
        return 2 * per_step + row_buffers * bm * d_in * 2

    bm = next(c for c in (2048, 1024, 512, 256, 128)
              if m % c == 0 and vmem_bytes(c) <= DENSE_WINDOW_BUDGET_BYTES)
    row_mode = dict(pipeline_mode=pl.Buffered(1)) if single_buffer_rows else {}
    in_specs = [pl.BlockSpec((bm, x.shape[1]), lambda i, j: (i, 0), **row_mode) for x in xs]
    row_block = 0
    for x in xs:
        kp = x.shape[1]
        in_specs.append(pl.BlockSpec((None, kp, bn), functools.partial(
            lambda i, j, rb: (layer, rb, j), rb=row_block // kp)))
        row_block += kp
    args = list(xs) + [w_stack] * len(xs)
    if epilogue == "residual":
        in_specs.append(pl.BlockSpec((bm, bn), lambda i, j: (i, j)))
        args.append(residual)
    out_specs = pl.BlockSpec((bm, bn), lambda i, j: (i, j))
    out_shape = jax.ShapeDtypeStruct((m, n), out_dtype)
    if with_meta:
        in_specs += [pl.BlockSpec((META_ROWS, x.shape[1]), lambda i, j: (0, 0)) for x in meta_xs]
        args += list(meta_xs)
        if epilogue == "residual":
            in_specs.append(pl.BlockSpec((META_ROWS, bn), lambda i, j: (0, _meta_col(i, j))))
            args.append(meta_residual)
        mspec, mshape = _meta_out(n, bn, out_dtype)
        mspec = mspec(lambda i, j: (jnp.minimum(i, 1), _meta_col(i, j)))
        out_specs, out_shape = [out_specs, mspec], [out_shape, mshape]
    out = pl.pallas_call(
        functools.partial(_matmul_kernel, n_parts=len(xs), epilogue=epilogue, with_meta=with_meta),
        grid=(m // bm, n // bn),
        in_specs=in_specs,
        out_specs=out_specs,
        out_shape=out_shape,
        compiler_params=_params(("arbitrary", "arbitrary") if with_meta else ("parallel", "parallel")),
        name="matmul_" + (epilogue or "plain"),
    )(*args)
    return (out[0], out[1][:META_ROWS]) if with_meta else out


def _matmul_kgrid_kernel(*refs, with_meta):
    if with_meta:
        x_ref, w_ref, r_ref, xm_ref, rm_ref, o_ref, om_ref = refs
    else:
        x_ref, w_ref, r_ref, o_ref = refs
    i, k = pl.program_id(0), pl.program_id(2)

    @pl.when(k == 0)
    def _():
        o_ref[...] = r_ref[...]
        if with_meta:
            om_ref[...] = jnp.where(i == 0, rm_ref[...], 0.0)

    o_ref[...] += jnp.dot(x_ref[...], w_ref[...].astype(BF16), preferred_element_type=F32)

    if with_meta:
        @pl.when(i == 0)
        def _():
            om_ref[...] += jnp.dot(xm_ref[...], w_ref[...].astype(BF16), preferred_element_type=F32)


def matmul_residual_kgrid(x, w_stack, layer, residual, meta_x=None, meta_residual=None):
    m, kdim = x.shape
    n = w_stack.shape[2]
    with_meta = meta_x is not None
    bm = _pick(m, (1024, 512, 256, 128))
    bn = _pick(n, (1024, 512, 256, 128))
    bk = _pick(kdim, (2048, 1024, 512, 256, 128))
    tile = pl.BlockSpec((bm, bn), lambda i, j, k: (i, j))
    in_specs = [
        pl.BlockSpec((bm, bk), lambda i, j, k: (i, k)),
        pl.BlockSpec((None, bk, bn), lambda i, j, k: (layer, k, j)),
        tile,
    ]
    args = [x, w_stack, residual]
    out_specs, out_shape = tile, jax.ShapeDtypeStruct((m, n), F32)
    if with_meta:
        in_specs += [pl.BlockSpec((META_ROWS, bk), lambda i, j, k: (0, _meta_col(i, k))),
                     pl.BlockSpec((META_ROWS, bn), lambda i, j, k: (0, _meta_col(i, j)))]
        args += [meta_x, meta_residual]
        mspec, mshape = _meta_out(n, bn, F32)
        mspec = mspec(lambda i, j, k: (jnp.minimum(i, 1), _meta_col(i, j)))
        out_specs, out_shape = [out_specs, mspec], [out_shape, mshape]
    out = pl.pallas_call(
        functools.partial(_matmul_kgrid_kernel, with_meta=with_meta),
        grid=(m // bm, n // bn, kdim // bk),
        in_specs=in_specs,
        out_specs=out_specs,
        out_shape=out_shape,
        compiler_params=_params(("arbitrary",) * 3 if with_meta else ("parallel", "parallel", "arbitrary")),
        name="matmul_residual_kgrid",
    )(*args)
    return (out[0], out[1][:META_ROWS]) if with_meta else out


def _qkv_prep_kernel(q_ref, k_ref, v_ref, gq_ref, gk_ref, pool_ref, qo_ref, ko_ref, vt_ref):
    pool = pool_ref[...]
    width = q_ref.shape[2]
    for h in range(width // HEAD):
        cols = slice(h * HEAD, (h + 1) * HEAD)
        for src, g_ref, dst in ((q_ref, gq_ref, qo_ref), (k_ref, gk_ref, ko_ref)):
            x = src[0, :, cols].astype(F32)
            ms = jnp.dot((x * x).astype(BF16), pool, preferred_element_type=F32)
            dst[0, :, cols] = (x * lax.rsqrt(ms + EPS) * g_ref[...]).astype(dst.dtype)
        vt_ref[0, h, 0] = v_ref[0, :, cols].astype(F32).T.astype(vt_ref.dtype)


def qkv_prep(proj3, q_gain, k_gain, width, tile):
    batch, length, _ = proj3.shape
    n_heads = width // HEAD
    nblk = length // tile
    pool = np.kron(np.eye(2), np.full((HALF, HALF), 1.0 / HALF)).astype(np.float32)
    qk = jax.ShapeDtypeStruct((batch, length, width), BF16)
    vt = jax.ShapeDtypeStruct((batch, n_heads, nblk, HEAD, tile), BF16)
    gspec = pl.BlockSpec((1, HEAD), lambda b, i: (0, 0))
    qkspec = pl.BlockSpec((1, tile, width), lambda b, i: (b, i, 0))
    return pl.pallas_call(
        _qkv_prep_kernel,
        grid=(batch, nblk),
        in_specs=[
            pl.BlockSpec((1, tile, width), lambda b, i: (b, i, 4)),
            pl.BlockSpec((1, tile, width), lambda b, i: (b, i, 5)),
            pl.BlockSpec((1, tile, width), lambda b, i: (b, i, 6)),
            gspec, gspec,
            pl.BlockSpec((HEAD, HEAD), lambda b, i: (0, 0)),
        ],
        out_specs=[qkspec, qkspec,
                   pl.BlockSpec((1, n_heads, 1, HEAD, tile), lambda b, i: (b, 0, i, 0, 0))],
        out_shape=[qk, qk, vt],
        compiler_params=_params(("parallel", "parallel")),
        name="qkv_prep",
    )(proj3, proj3, proj3, q_gain, k_gain, jnp.asarray(pool, BF16))


def _alibi_slopes_log2(n_heads):
    return np.exp2(-8.0 * np.arange(1, n_heads + 1, dtype=np.float64) / n_heads) * LOG2E


def _attn_key_feat(n_heads, rows, mask_pad):
    ki = np.arange(rows)
    feat = (_alibi_slopes_log2(n_heads)[None, :] * ki[:, None]).astype(np.float32)
    if mask_pad:
        feat = np.where((ki < PAD)[:, None], np.float32(MASK_VALUE), feat)
    hi = feat.astype(BF16)
    lo = np.where(feat <= MASK_VALUE, np.float32(0.0), feat - hi.astype(np.float32)).astype(BF16)
    key_feat = np.zeros((rows, n_heads, HEAD), BF16)
    key_feat[..., 0] = hi
    key_feat[..., 1] = lo
    return jnp.asarray(key_feat.reshape(rows, n_heads * HEAD))


def _attn_diag_bias(n_heads, tile):
    slopes = jnp.asarray(_alibi_slopes_log2(n_heads), F32)
    ki = jnp.arange(tile, dtype=jnp.int32)
    kk, qq = ki[:, None], ki[None, :]
    allowed = (kk // ATTN_CHUNK) <= (qq // ATTN_CHUNK)
    rel = (-jnp.abs(qq - kk) + qq - kk).astype(F32)
    return jnp.where(allowed[None], slopes[:, None, None] * rel[None], MASK_VALUE)


def _ones_rows(cols):
    return jnp.broadcast_to((jnp.arange(ONES_ROWS) == 0).astype(BF16)[:, None], (ONES_ROWS, cols))


def _attn_kernel(*refs, tile, heads, with_meta, mask_pad):
    refs = list(refs)
    scal_ref, slope_ref, q_ref, k_ref, vt_ref, kfeat_ref, qfeat_ref, ones_ref, dbias_ref, g_ref = refs[:10]
    if with_meta:
        km_ref, vtm_ref, kfeatm_ref, onesm_ref = refs[10:14]
    o_ref, acc_ref, s_ref = refs[-3:]
    i = pl.program_id(2)
    lam = scal_ref[0]
    out_scale = scal_ref[1]
    qfeat = qfeat_ref[...]
    ones = ones_ref[...]
    lane = lax.broadcasted_iota(jnp.int32, (tile, HEAD), 1)

    def head_cols(h):
        return slice(h * HEAD, (h + 1) * HEAD)

    qcat = []
    for h in range(heads):
        q = q_ref[0, :, head_cols(h)]
        qcat.append(jnp.concatenate(
            [jnp.concatenate([jnp.where(keep, q, jnp.zeros_like(q)), qfeat], axis=1)
             for keep in (lane < HALF, lane >= HALF)], axis=0))
    slope = [slope_ref[pl.program_id(0) * heads + h] for h in range(heads)]
    block_shift = [s * float(tile) for s in slope]

    def scores(kb, feat, h):
        kaug = jnp.concatenate([kb, feat], axis=1)
        return lax.dot_general(kaug, qcat[h], (((1,), (1,)), ((), ())), preferred_element_type=F32)

    def produce(h, j, slot, diag=False):
        kb = k_ref[0, pl.ds(pl.multiple_of(j * tile, tile), tile), head_cols(h)]
        s = scores(kb, kfeat_ref[:, head_cols(h)], h)
        if diag:
            s = s + jnp.concatenate([dbias_ref[h]] * 2, axis=1)
        s_ref[h, slot] = s
        return jnp.max(s, axis=0, keepdims=True)

    def consume(h, j, slot, col_max, m_old):
        vaug = jnp.concatenate([vt_ref[0, h, j], ones], axis=0)
        shift = block_shift[h] * (j - i).astype(F32)
        m_new = jnp.maximum(m_old, col_max + shift)
        p = jnp.exp2(s_ref[h, slot] - (m_new - shift)).astype(BF16)
        acc_ref[h] = acc_ref[h] * jnp.exp2(m_old - m_new) + jnp.dot(vaug, p, preferred_element_type=F32)
        return m_new

    def start(h):
        if not with_meta:
            acc_ref[h] = jnp.zeros(acc_ref.shape[1:], F32)
            return jnp.full((1, 2 * tile), MASK_VALUE, F32)
        s = scores(km_ref[0, :, head_cols(h)], kfeatm_ref[:, head_cols(h)], h)
        shift = -(slope[h] * float(META_ROWS) + block_shift[h] * i.astype(F32))
        m_new = jnp.max(s, axis=0, keepdims=True) + shift
        p = jnp.exp2(s - (m_new - shift)).astype(BF16)
        vaug = jnp.concatenate([vtm_ref[0, h, 0], onesm_ref[...]], axis=0)
        acc_ref[h] = jnp.dot(vaug, p, preferred_element_type=F32)
        return m_new

    init = tuple((start(h), produce(h, i, 0, diag=True)) for h in range(heads))

    def pipe(t, slot, carry):
        prev = jnp.where(t == 0, i, t - 1)
        return tuple((consume(h, prev, slot, carry[h][1], carry[h][0]), produce(h, t, 1 - slot))
                     for h in range(heads))

    carry = lax.fori_loop(0, i // 2, lambda n, c: pipe(2 * n + 1, 1, pipe(2 * n, 0, c)), init)

    def odd_tail(c):
        c = pipe(i - 1, 0, c)
        for h in range(heads):
            consume(h, i - 1, 1, c[h][1], c[h][0])
        return 0

    def even_tail(c):
        for h in range(heads):
            consume(h, jnp.where(i == 0, i, i - 1), 0, c[h][1], c[h][0])
        return 0

    lax.cond(i % 2 == 1, odd_tail, even_tail, carry)

    row = lax.broadcasted_iota(jnp.int32, (tile, HEAD), 0) + i * tile
    for h in range(heads):
        def normalized(c):
            cols = slice(c * tile, (c + 1) * tile)
            return acc_ref[h, :HEAD, cols] / acc_ref[h, HEAD:HEAD + 1, cols]

        o = normalized(0) - lam * normalized(1)
        o = o * lax.rsqrt(jnp.mean(o * o, axis=0, keepdims=True) + EPS)
        ot = o.T * (g_ref[...] * out_scale)
        if mask_pad:
            ot = jnp.where(row >= PAD, ot, 0.0)
        o_ref[0, :, head_cols(h)] = ot.astype(o_ref.dtype)


def diff_attention(qn, kn, vt, lam, lam_init, out_g, tile, meta=None):
    batch, length, width = qn.shape
    n_heads = width // HEAD
    heads = _pick(n_heads, (ATTN_HEADS_PER_STEP, 1))
    nq = length // tile
    with_meta = meta is not None
    scal = jnp.stack([lam.astype(F32), jnp.asarray(1.0 - lam_init, F32)])
    query_feat = jnp.broadcast_to((jnp.arange(HEAD) < 2).astype(BF16)[None, :], (tile, HEAD))
    smem = pl.BlockSpec(memory_space=pltpu.SMEM)
    bw = heads * HEAD
    in_specs = [
        smem, smem,
        pl.BlockSpec((1, tile, bw), lambda g, b, i: (b, i, g)),
        pl.BlockSpec((1, length, bw), lambda g, b, i: (b, 0, g)),
        pl.BlockSpec((1, heads, nq, HEAD, tile), lambda g, b, i: (b, g, 0, 0, 0)),
        pl.BlockSpec((tile, bw), lambda g, b, i: (0, g)),
        pl.BlockSpec((tile, HEAD), lambda g, b, i: (0, 0)),
        pl.BlockSpec((ONES_ROWS, tile), lambda g, b, i: (0, 0)),
        pl.BlockSpec((heads, tile, tile), lambda g, b, i: (g, 0, 0)),
        pl.BlockSpec((1, HEAD), lambda g, b, i: (0, 0)),
    ]
    args = [scal, jnp.asarray(_alibi_slopes_log2(n_heads), F32), qn, kn, vt,
            _attn_key_feat(n_heads, tile, mask_pad=not with_meta), query_feat, _ones_rows(tile),
            _attn_diag_bias(n_heads, tile), out_g.reshape(1, HEAD).astype(F32)]
    if with_meta:
        in_specs += [
            pl.BlockSpec((1, N_META, bw), lambda g, b, i: (0, 0, g)),
            pl.BlockSpec((1, heads, 1, HEAD, N_META), lambda g, b, i: (0, g, 0, 0, 0)),
            pl.BlockSpec((N_META, bw), lambda g, b, i: (0, g)),
            pl.BlockSpec((ONES_ROWS, N_META), lambda g, b, i: (0, 0)),
        ]
        args += [meta[0][:, PAD:], meta[1][..., PAD:],
                 _attn_key_feat(n_heads, META_ROWS, mask_pad=False)[PAD:], _ones_rows(N_META)]
    return pl.pallas_call(
        functools.partial(_attn_kernel, tile=tile, heads=heads, with_meta=with_meta, mask_pad=not with_meta),
        grid=(n_heads // heads, batch, nq),
        in_specs=in_specs,
        out_specs=pl.BlockSpec((1, tile, bw), lambda g, b, i: (b, i, g)),
        out_shape=jax.ShapeDtypeStruct((batch, length, width), BF16),
        scratch_shapes=[pltpu.VMEM((heads, HEAD + ONES_ROWS, 2 * tile), F32),
                        pltpu.VMEM((heads, 2, tile, 2 * tile), F32)],
        compiler_params=_params(("parallel", "parallel", "parallel")),
        name="diff_attention",
    )(*args)


def _hgrn_constants():
    c = HGRN_CHUNK
    t = np.arange(c)[:, None]
    r = np.arange(c)[None, :]
    sums = []
    pair = [np.eye(c)]
    for level in range(1, HGRN_LEVELS + 1):
        size = 2 ** level
        mid = (t // size) * size + size // 2 - 1
        upper = (t % size) >= size // 2
        sums.append(np.where(upper, (r > mid) & (r <= t), (r > t) & (r <= mid)))
        same = (t // size) == (r // size)
        pair.append(same & upper & ((r % size) < size // 2))
    sums.append(r <= t)
    sums.append(r > t)
    sums = np.concatenate(sums, axis=0).astype(np.float32)
    return (jnp.asarray(np.concatenate([sums, sums], axis=1), BF16),
            jnp.asarray(np.stack(pair).astype(np.float32)))


def _hgrn_kernel(*refs, heads, chunks, with_init, emit_state):
    refs = list(refs)
    q_ref, f_ref, v_ref, g_ref, lb_ref, og_ref, sums_ref, pair_ref = refs[:8]
    init_ref = refs[8] if with_init else None
    o_ref = refs[8 + with_init]
    state_out_ref = refs[9 + with_init] if emit_state else None
    state_ref = refs[-1]
    c = HGRN_CHUNK

    @pl.when(pl.program_id(2) == 0)
    def _():
        state_ref[...] = init_ref[...] if with_init else jnp.zeros_like(state_ref)

    def chunk_body(n):
        rows = slice(n * c, (n + 1) * c)
        lb_all = lb_ref[...]
        sig_all = jax.nn.sigmoid(f_ref[0, rows, :].astype(F32))
        logf = jnp.log2(lb_all + (1.0 - lb_all) * sig_all)
        k_all = ((1.0 - lb_all) * (1.0 - sig_all)).astype(BF16)
        hi = logf.astype(BF16)
        lo = (logf - hi.astype(F32)).astype(BF16)
        decay_all = jnp.exp2(jnp.dot(sums_ref[...], jnp.concatenate([hi, lo], axis=0),
                                     preferred_element_type=F32))
        for h in range(heads):
            cols = slice(h * HEAD, (h + 1) * HEAD)
            q = q_ref[0, rows, cols].astype(BF16)
            v = v_ref[0, rows, cols].astype(BF16)
            gate = g_ref[0, rows, cols].astype(F32)
            k = k_all[:, cols]
            decay = decay_all[:, cols]

            a = pair_ref[0] * lax.dot_general(q, k, (((1,), (1,)), ((), ())), preferred_element_type=F32)
            for level in range(1, HGRN_LEVELS + 1):
                d = decay[(level - 1) * c:level * c].astype(BF16)
                a = a + pair_ref[level] * lax.dot_general(q * d, k * d, (((1,), (1,)), ((), ())),
                                                          preferred_element_type=F32)
            from_start = decay[7 * c:8 * c]
            to_end = decay[8 * c:9 * c].astype(BF16)
            state_t = state_ref[h]
            o = jnp.dot(a.astype(BF16), v, preferred_element_type=F32)
            o = o + lax.dot_general(q * from_start.astype(BF16), state_t.astype(BF16),
                                    (((1,), (1,)), ((), ())), preferred_element_type=F32)
            state_ref[h] = state_t * from_start[c - 1:c, :] + lax.dot_general(
                v, k * to_end, (((0,), (0,)), ((), ())), preferred_element_type=F32)

            y = o * lax.rsqrt(jnp.mean(o * o, axis=-1, keepdims=True) + EPS) * og_ref[...]
            o_ref[0, rows, cols] = (y * (gate * jax.nn.sigmoid(gate))).astype(o_ref.dtype)

    for n in range(chunks):
        chunk_body(n)

    if emit_state:
        @pl.when(pl.program_id(2) == pl.num_programs(2) - 1)
        def _():
            state_out_ref[0] = state_ref[...]


def hgrn2(proj3, lb, out_g, width, init_state=None, emit_state=False):
    batch, length, _ = proj3.shape
    n_heads = width // HEAD
    heads = _pick(n_heads, (4, 2, 1))
    tile = _pick(length, (1024, 512, 384, 256, 128))
    groups = n_heads // heads
    bw = heads * HEAD
    sums, pair = _hgrn_constants()
    with_init = init_state is not None

    def seg_spec(seg):
        return pl.BlockSpec((1, tile, bw), lambda b, hg, t: (b, t, seg * groups + hg))

    in_specs = [
        seg_spec(0), seg_spec(1), seg_spec(2), seg_spec(3),
        pl.BlockSpec((1, bw), lambda b, hg, t: (0, hg)),
        pl.BlockSpec((1, HEAD), lambda b, hg, t: (0, 0)),
        pl.BlockSpec(sums.shape, lambda b, hg, t: (0, 0)),
        pl.BlockSpec(pair.shape, lambda b, hg, t: (0, 0, 0)),
    ]
    args = [proj3, proj3, proj3, proj3, lb.reshape(1, width).astype(F32),
            out_g.reshape(1, HEAD).astype(F32), sums, pair]
    if with_init:
        in_specs.append(pl.BlockSpec((heads, HEAD, HEAD), lambda b, hg, t: (hg, 0, 0)))
        args.append(init_state)
    out_specs = pl.BlockSpec((1, tile, bw), lambda b, hg, t: (b, t, hg))
    out_shape = jax.ShapeDtypeStruct((batch, length, width), BF16)
    if emit_state:
        out_specs = [out_specs, pl.BlockSpec((1, heads, HEAD, HEAD), lambda b, hg, t: (b, hg, 0, 0))]
        out_shape = [out_shape, jax.ShapeDtypeStruct((batch, n_heads, HEAD, HEAD), F32)]
    return pl.pallas_call(
        functools.partial(_hgrn_kernel, heads=heads, chunks=tile // HGRN_CHUNK, with_init=with_init,
                          emit_state=emit_state),
        grid=(batch, groups, length // tile),
        in_specs=in_specs,
        out_specs=out_specs,
        out_shape=out_shape,
        scratch_shapes=[pltpu.VMEM((heads, HEAD, HEAD), F32)],
        compiler_params=_params(("parallel", "parallel", "arbitrary")),
        name="hgrn2",
    )(*args)


def kernel(x, meta_tokens, norm1_g, w_in, hgrn_lb_raw, hgrn_out_g, q_norm_g, k_norm_g, diff_lambda,
           diff_out_g, w_out, norm2_g, w_mlp_up, w_mlp_down):
    batch, seq, d_model = x.shape
    depth = w_in.shape[0]
    width = d_model // 2
    tile = _pick(seq, (512, 384, 256, 128))
    h = x.reshape(batch * seq, d_model)
    hm = jnp.concatenate([jnp.zeros((PAD, d_model), x.dtype), meta_tokens.astype(x.dtype)], axis=0)

    lb_all = jnp.cumsum(jax.nn.softmax(hgrn_lb_raw.astype(F32), axis=0), axis=0)
    lb_all = lb_all - lb_all[0:1]
    q_scale = LOG2E / math.sqrt(HALF)

    for layer in range(depth):
        keep_meta = layer + 1 < depth
        q_gain = (q_norm_g[layer].reshape(1, HEAD) * q_scale).astype(F32)
        k_gain = k_norm_g[layer].reshape(1, HEAD).astype(F32)
        lp = diff_lambda[layer].astype(F32)
        lam_init = 0.8 - 0.6 * math.exp(-0.3 * layer)
        lam = jnp.exp(jnp.sum(lp[0] * lp[1])) - jnp.exp(jnp.sum(lp[2] * lp[3])) + lam_init

        proj, projm = matmul([rmsnorm(h, norm1_g[layer])], w_in, layer, BF16,
                             meta_xs=[rmsnorm(hm, norm1_g[layer])], single_buffer_rows=True)
        proj3 = proj.reshape(batch, seq, -1)
        projm3 = projm.reshape(1, META_ROWS, -1)
        o_am, state = hgrn2(projm3, lb_all[layer], hgrn_out_g[layer], width, emit_state=True)
        o_a = hgrn2(proj3, lb_all[layer], hgrn_out_g[layer], width, init_state=state[0])
        qn_m, kn_m, vt_m = qkv_prep(projm3, q_gain, k_gain, width, META_ROWS)
        qn, kn, vt = qkv_prep(proj3, q_gain, k_gain, width, tile)
        o_b = diff_attention(qn, kn, vt, lam, lam_init, diff_out_g[layer], tile, meta=(kn_m, vt_m))
        mixed = [o_a.reshape(batch * seq, width), o_b.reshape(batch * seq, width)]
        if keep_meta:
            o_bm = diff_attention(qn_m, kn_m, vt_m, lam, lam_init, diff_out_g[layer], META_ROWS)
            h, hm = matmul(mixed, w_out, layer, F32, epilogue="residual", residual=h,
                           meta_xs=[o_am.reshape(META_ROWS, width), o_bm.reshape(META_ROWS, width)],
                           meta_residual=hm)
            z, zm = matmul([rmsnorm(h, norm2_g[layer])], w_mlp_up, layer, BF16, epilogue="relu2",
                           meta_xs=[rmsnorm(hm, norm2_g[layer])], single_buffer_rows=True)
            h, hm = matmul_residual_kgrid(z, w_mlp_down, layer, h, meta_x=zm, meta_residual=hm)
        else:
            h = matmul(mixed, w_out, layer, F32, epilogue="residual", residual=h)
            z = matmul([rmsnorm(h, norm2_g[layer])], w_mlp_up, layer, BF16, epilogue="relu2",
                       single_buffer_rows=True)
            h = matmul_residual_kgrid(z, w_mlp_down, layer, h)
    return h.reshape(batch, seq, d_model)
```

```python
import functools
import math

import numpy as np
import jax
import jax.numpy as jnp
from jax import lax
from jax.experimental import pallas as pl
from jax.experimental.pallas import tpu as pltpu

N_META = 16
ATTN_CHUNK = 64
HEAD = 128
HALF = HEAD // 2
META_ROWS = 128
PAD = META_ROWS - N_META
HGRN_CHUNK = 128
HGRN_LEVELS = 7
ONES_ROWS = 16
ATTN_HEADS_PER_STEP = 4
EPS = 1e-6
MASK_VALUE = -1e30
LOG2E = math.log2(math.e)
V7X_VMEM_LIMIT_BYTES = 56 * 1024 * 1024
DENSE_WINDOW_BUDGET_BYTES = 50 * 1024 * 1024

F32 = jnp.float32
BF16 = jnp.bfloat16


def _params(semantics):
    return pltpu.CompilerParams(dimension_semantics=semantics, vmem_limit_bytes=V7X_VMEM_LIMIT_BYTES)


def _pick(n, candidates):
    for c in candidates:
        if n % c == 0:
            return c
    raise ValueError(f"no tile for {n} among {candidates}")


def _rmsnorm_kernel(x_ref, g_ref, o_ref):
    x = x_ref[...]
    ms = jnp.mean(x * x, axis=-1, keepdims=True)
    o_ref[...] = (x * lax.rsqrt(ms + EPS) * g_ref[...]).astype(o_ref.dtype)


def rmsnorm(x, g):
    m, d = x.shape
    bm = _pick(m, (512, 256, 128))
    return pl.pallas_call(
        _rmsnorm_kernel,
        grid=(m // bm,),
        in_specs=[pl.BlockSpec((bm, d), lambda i: (i, 0)), pl.BlockSpec((1, d), lambda i: (0, 0))],
        out_specs=pl.BlockSpec((bm, d), lambda i: (i, 0)),
        out_shape=jax.ShapeDtypeStruct((m, d), BF16),
        compiler_params=_params(("parallel",)),
        name="rmsnorm",
    )(x, g.reshape(1, d).astype(F32))


def _epilogue(acc, epilogue, r_ref):
    if epilogue == "relu2":
        acc = jnp.square(jnp.maximum(acc, 0.0))
    if epilogue == "residual":
        acc = acc + r_ref[...]
    return acc


def _matmul_kernel(*refs, n_parts, epilogue, with_meta, aliased):
    refs = list(refs)
    take = lambda n: [refs.pop(0) for _ in range(n)]
    xs, ws = take(n_parts), take(n_parts)
    r_ref = refs.pop(0) if epilogue == "residual" else None
    xms = take(n_parts) if with_meta else []
    rm_ref = refs.pop(0) if with_meta and epilogue == "residual" else None
    if aliased:
        refs.pop(0)
    o_ref = refs.pop(0)

    def product(x_refs, res_ref):
        acc = jnp.dot(x_refs[0][...], ws[0][...].astype(BF16), preferred_element_type=F32)
        for x_ref, w_ref in zip(x_refs[1:], ws[1:]):
            acc = acc + jnp.dot(x_ref[...], w_ref[...].astype(BF16), preferred_element_type=F32)
        return _epilogue(acc, epilogue, res_ref)

    o_ref[...] = product(xs, r_ref).astype(o_ref.dtype)
    if with_meta:
        om_ref = refs.pop(0)
        om_ref[...] = product(xms, rm_ref).astype(om_ref.dtype)


def _fill_in(call, rest_tiles):
    if rest_tiles == 0:
        return call(0, 1, None, True)
    return call(0, 1, call(1, rest_tiles, None, False), True)


def matmul(xs, w_stack, layer, out_dtype, epilogue=None, residual=None, meta_xs=None, meta_residual=None,
           single_buffer_rows=False):
    m = xs[0].shape[0]
    n = w_stack.shape[2]
    d_in = sum(x.shape[1] for x in xs)
    bn = _pick(n, (512, 256, 128))

    row_buffers = 1 if single_buffer_rows else 2

    def vmem_bytes(bm):
        per_step = d_in * bn * 4 + bm * bn * jnp.dtype(out_dtype).itemsize
        if epilogue == "residual":
            per_step += bm * bn * 4
        return 2 * per_step + row_buffers * bm * d_in * 2

    bm = next(c for c in (2048, 1024, 512, 256, 128)
              if m % c == 0 and vmem_bytes(c) <= DENSE_WINDOW_BUDGET_BYTES)
    row_mode = dict(pipeline_mode=pl.Buffered(1)) if single_buffer_rows else {}

    def call(first_tile, tiles, buffer, with_meta):
        in_specs = [pl.BlockSpec((bm, x.shape[1]), lambda i, j: (i + first_tile, 0), **row_mode) for x in xs]
        row_block = 0
        for x in xs:
            kp = x.shape[1]
            in_specs.append(pl.BlockSpec((None, kp, bn), functools.partial(
                lambda i, j, rb: (layer, rb, j), rb=row_block // kp)))
            row_block += kp
        args = list(xs) + [w_stack] * len(xs)
        if epilogue == "residual":
            in_specs.append(pl.BlockSpec((bm, bn), lambda i, j: (i + first_tile, j)))
            args.append(residual)
        out_specs = pl.BlockSpec((bm, bn), lambda i, j: (i + first_tile, j))
        out_shape = jax.ShapeDtypeStruct((m, n), out_dtype)
        if with_meta:
            in_specs += [pl.BlockSpec((META_ROWS, x.shape[1]), lambda i, j: (0, 0)) for x in meta_xs]
            args += list(meta_xs)
            if epilogue == "residual":
                in_specs.append(pl.BlockSpec((META_ROWS, bn), lambda i, j: (0, j)))
                args.append(meta_residual)
            out_specs = [out_specs, pl.BlockSpec((META_ROWS, bn), lambda i, j: (0, j))]
            out_shape = [out_shape, jax.ShapeDtypeStruct((META_ROWS, n), out_dtype)]
        aliases = {}
        if buffer is not None:
            in_specs.append(pl.BlockSpec(memory_space=pl.ANY))
            args.append(buffer)
            aliases = {len(args) - 1: 0}
        return pl.pallas_call(
            functools.partial(_matmul_kernel, n_parts=len(xs), epilogue=epilogue, with_meta=with_meta,
                              aliased=buffer is not None),
            grid=(tiles, n // bn),
            in_specs=in_specs,
            out_specs=out_specs,
            out_shape=out_shape,
            input_output_aliases=aliases,
            compiler_params=_params(("parallel", "parallel")),
            name="matmul_" + (epilogue or "plain"),
        )(*args)

    if meta_xs is None:
        return call(0, m // bm, None, False)
    return _fill_in(call, m // bm - 1)


def _matmul_kgrid_kernel(*refs, with_meta, aliased):
    refs = list(refs)
    x_ref, w_ref, r_ref = refs[:3]
    if with_meta:
        xm_ref, rm_ref = refs[3:5]
    outs = refs[3 + 2 * with_meta + aliased:]
    o_ref = outs[0]

    @pl.when(pl.program_id(2) == 0)
    def _():
        o_ref[...] = r_ref[...]
        if with_meta:
            outs[1][...] = rm_ref[...]

    o_ref[...] += jnp.dot(x_ref[...], w_ref[...].astype(BF16), preferred_element_type=F32)
    if with_meta:
        outs[1][...] += jnp.dot(xm_ref[...], w_ref[...].astype(BF16), preferred_element_type=F32)


def matmul_residual_kgrid(x, w_stack, layer, residual, meta_x=None, meta_residual=None):
    m, kdim = x.shape
    n = w_stack.shape[2]
    bm = _pick(m, (1024, 512, 256, 128))
    bn = _pick(n, (1024, 512, 256, 128))
    bk = _pick(kdim, (2048, 1024, 512, 256, 128))

    def call(first_tile, tiles, buffer, with_meta):
        tile = pl.BlockSpec((bm, bn), lambda i, j, k: (i + first_tile, j))
        in_specs = [
            pl.BlockSpec((bm, bk), lambda i, j, k: (i + first_tile, k)),
            pl.BlockSpec((None, bk, bn), lambda i, j, k: (layer, k, j)),
            tile,
        ]
        args = [x, w_stack, residual]
        out_specs, out_shape = tile, jax.ShapeDtypeStruct((m, n), F32)
        if with_meta:
            in_specs += [pl.BlockSpec((META_ROWS, bk), lambda i, j, k: (0, k)),
                         pl.BlockSpec((META_ROWS, bn), lambda i, j, k: (0, j))]
            args += [meta_x, meta_residual]
            out_specs = [out_specs, pl.BlockSpec((META_ROWS, bn), lambda i, j, k: (0, j))]
            out_shape = [out_shape, jax.ShapeDtypeStruct((META_ROWS, n), F32)]
        aliases = {}
        if buffer is not None:
            in_specs.append(pl.BlockSpec(memory_space=pl.ANY))
            args.append(buffer)
            aliases = {len(args) - 1: 0}
        return pl.pallas_call(
            functools.partial(_matmul_kgrid_kernel, with_meta=with_meta, aliased=buffer is not None),
            grid=(tiles, n // bn, kdim // bk),
            in_specs=in_specs,
            out_specs=out_specs,
            out_shape=out_shape,
            input_output_aliases=aliases,
            compiler_params=_params(("parallel", "parallel", "arbitrary")),
            name="matmul_residual_kgrid",
        )(*args)

    if meta_x is None:
        return call(0, m // bm, None, False)
    return _fill_in(call, m // bm - 1)


def _qkv_prep_kernel(q_ref, k_ref, v_ref, gq_ref, gk_ref, pool_ref, qo_ref, ko_ref, vt_ref):
    pool = pool_ref[...]
    width = q_ref.shape[2]
    for h in range(width // HEAD):
        cols = slice(h * HEAD, (h + 1) * HEAD)
        for src, g_ref, dst in ((q_ref, gq_ref, qo_ref), (k_ref, gk_ref, ko_ref)):
            x = src[0, :, cols].astype(F32)
            ms = jnp.dot((x * x).astype(BF16), pool, preferred_element_type=F32)
            dst[0, :, cols] = (x * lax.rsqrt(ms + EPS) * g_ref[...]).astype(dst.dtype)
        vt_ref[0, h, 0] = v_ref[0, :, cols].astype(F32).T.astype(vt_ref.dtype)


def qkv_prep(proj3, q_gain, k_gain, width, tile):
    batch, length, _ = proj3.shape
    n_heads = width // HEAD
    nblk = length // tile
    pool = np.kron(np.eye(2), np.full((HALF, HALF), 1.0 / HALF)).astype(np.float32)
    qk = jax.ShapeDtypeStruct((batch, length, width), BF16)
    vt = jax.ShapeDtypeStruct((batch, n_heads, nblk, HEAD, tile), BF16)
    gspec = pl.BlockSpec((1, HEAD), lambda b, i: (0, 0))
    qkspec = pl.BlockSpec((1, tile, width), lambda b, i: (b, i, 0))
    return pl.pallas_call(
        _qkv_prep_kernel,
        grid=(batch, nblk),
        in_specs=[
            pl.BlockSpec((1, tile, width), lambda b, i: (b, i, 4)),
            pl.BlockSpec((1, tile, width), lambda b, i: (b, i, 5)),
            pl.BlockSpec((1, tile, width), lambda b, i: (b, i, 6)),
            gspec, gspec,
            pl.BlockSpec((HEAD, HEAD), lambda b, i: (0, 0)),
        ],
        out_specs=[qkspec, qkspec,
                   pl.BlockSpec((1, n_heads, 1, HEAD, tile), lambda b, i: (b, 0, i, 0, 0))],
        out_shape=[qk, qk, vt],
        compiler_params=_params(("parallel", "parallel")),
        name="qkv_prep",
    )(proj3, proj3, proj3, q_gain, k_gain, jnp.asarray(pool, BF16))


def _alibi_slopes_log2(n_heads):
    return np.exp2(-8.0 * np.arange(1, n_heads + 1, dtype=np.float64) / n_heads) * LOG2E


def _attn_key_feat(n_heads, rows, mask_pad):
    ki = np.arange(rows)
    feat = (_alibi_slopes_log2(n_heads)[None, :] * ki[:, None]).astype(np.float32)
    if mask_pad:
        feat = np.where((ki < PAD)[:, None], np.float32(MASK_VALUE), feat)
    hi = feat.astype(BF16)
    lo = np.where(feat <= MASK_VALUE, np.float32(0.0), feat - hi.astype(np.float32)).astype(BF16)
    key_feat = np.zeros((rows, n_heads, HEAD), BF16)
    key_feat[..., 0] = hi
    key_feat[..., 1] = lo
    return jnp.asarray(key_feat.reshape(rows, n_heads * HEAD))


def _attn_diag_bias(n_heads, tile):
    slopes = jnp.asarray(_alibi_slopes_log2(n_heads), F32)
    ki = jnp.arange(tile, dtype=jnp.int32)
    kk, qq = ki[:, None], ki[None, :]
    allowed = (kk // ATTN_CHUNK) <= (qq // ATTN_CHUNK)
    rel = (-jnp.abs(qq - kk) + qq - kk).astype(F32)
    return jnp.where(allowed[None], slopes[:, None, None] * rel[None], MASK_VALUE)


def _ones_rows(cols):
    return jnp.broadcast_to((jnp.arange(ONES_ROWS) == 0).astype(BF16)[:, None], (ONES_ROWS, cols))


def _attn_kernel(*refs, tile, heads, with_meta, mask_pad):
    refs = list(refs)
    scal_ref, slope_ref, q_ref, k_ref, vt_ref, kfeat_ref, qfeat_ref, ones_ref, dbias_ref, g_ref = refs[:10]
    if with_meta:
        km_ref, vtm_ref, kfeatm_ref, onesm_ref = refs[10:14]
    o_ref, acc_ref, s_ref = refs[-3:]
    i = pl.program_id(2)
    lam = scal_ref[0]
    out_scale = scal_ref[1]
    qfeat = qfeat_ref[...]
    ones = ones_ref[...]
    lane = lax.broadcasted_iota(jnp.int32, (tile, HEAD), 1)

    def head_cols(h):
        return slice(h * HEAD, (h + 1) * HEAD)

    qcat = []
    for h in range(heads):
        q = q_ref[0, :, head_cols(h)]
        qcat.append(jnp.concatenate(
            [jnp.concatenate([jnp.where(keep, q, jnp.zeros_like(q)), qfeat], axis=1)
             for keep in (lane < HALF, lane >= HALF)], axis=0))
    slope = [slope_ref[pl.program_id(0) * heads + h] for h in range(heads)]
    block_shift = [s * float(tile) for s in slope]

    def scores(kb, feat, h):
        kaug = jnp.concatenate([kb, feat], axis=1)
        return lax.dot_general(kaug, qcat[h], (((1,), (1,)), ((), ())), preferred_element_type=F32)

    def produce(h, j, slot, diag=False):
        kb = k_ref[0, pl.ds(pl.multiple_of(j * tile, tile), tile), head_cols(h)]
        s = scores(kb, kfeat_ref[:, head_cols(h)], h)
        if diag:
            s = s + jnp.concatenate([dbias_ref[h]] * 2, axis=1)
        s_ref[h, slot] = s
        return jnp.max(s, axis=0, keepdims=True)

    def consume(h, j, slot, col_max, m_old):
        vaug = jnp.concatenate([vt_ref[0, h, j], ones], axis=0)
        shift = block_shift[h] * (j - i).astype(F32)
        m_new = jnp.maximum(m_old, col_max + shift)
        p = jnp.exp2(s_ref[h, slot] - (m_new - shift)).astype(BF16)
        acc_ref[h] = acc_ref[h] * jnp.exp2(m_old - m_new) + jnp.dot(vaug, p, preferred_element_type=F32)
        return m_new

    def start(h):
        if not with_meta:
            acc_ref[h] = jnp.zeros(acc_ref.shape[1:], F32)
            return jnp.full((1, 2 * tile), MASK_VALUE, F32)
        s = scores(km_ref[0, :, head_cols(h)], kfeatm_ref[:, head_cols(h)], h)
        shift = -(slope[h] * float(META_ROWS) + block_shift[h] * i.astype(F32))
        m_new = jnp.max(s, axis=0, keepdims=True) + shift
        p = jnp.exp2(s - (m_new - shift)).astype(BF16)
        vaug = jnp.concatenate([vtm_ref[0, h, 0], onesm_ref[...]], axis=0)
        acc_ref[h] = jnp.dot(vaug, p, preferred_element_type=F32)
        return m_new

    init = tuple((start(h), produce(h, i, 0, diag=True)) for h in range(heads))

    def pipe(t, slot, carry):
        prev = jnp.where(t == 0, i, t - 1)
        return tuple((consume(h, prev, slot, carry[h][1], carry[h][0]), produce(h, t, 1 - slot))
                     for h in range(heads))

    carry = lax.fori_loop(0, i // 2, lambda n, c: pipe(2 * n + 1, 1, pipe(2 * n, 0, c)), init)

    def odd_tail(c):
        c = pipe(i - 1, 0, c)
        for h in range(heads):
            consume(h, i - 1, 1, c[h][1], c[h][0])
        return 0

    def even_tail(c):
        for h in range(heads):
            consume(h, jnp.where(i == 0, i, i - 1), 0, c[h][1], c[h][0])
        return 0

    lax.cond(i % 2 == 1, odd_tail, even_tail, carry)

    row = lax.broadcasted_iota(jnp.int32, (tile, HEAD), 0) + i * tile
    for h in range(heads):
        def normalized(c):
            cols = slice(c * tile, (c + 1) * tile)
            return acc_ref[h, :HEAD, cols] / acc_ref[h, HEAD:HEAD + 1, cols]

        o = normalized(0) - lam * normalized(1)
        o = o * lax.rsqrt(jnp.mean(o * o, axis=0, keepdims=True) + EPS)
        ot = o.T * (g_ref[...] * out_scale)
        if mask_pad:
            ot = jnp.where(row >= PAD, ot, 0.0)
        o_ref[0, :, head_cols(h)] = ot.astype(o_ref.dtype)


def diff_attention(qn, kn, vt, lam, lam_init, out_g, tile, meta=None):
    batch, length, width = qn.shape
    n_heads = width // HEAD
    heads = _pick(n_heads, (ATTN_HEADS_PER_STEP, 1))
    nq = length // tile
    with_meta = meta is not None
    scal = jnp.stack([lam.astype(F32), jnp.asarray(1.0 - lam_init, F32)])
    query_feat = jnp.broadcast_to((jnp.arange(HEAD) < 2).astype(BF16)[None, :], (tile, HEAD))
    smem = pl.BlockSpec(memory_space=pltpu.SMEM)
    bw = heads * HEAD
    in_specs = [
        smem, smem,
        pl.BlockSpec((1, tile, bw), lambda g, b, i: (b, i, g)),
        pl.BlockSpec((1, length, bw), lambda g, b, i: (b, 0, g)),
        pl.BlockSpec((1, heads, nq, HEAD, tile), lambda g, b, i: (b, g, 0, 0, 0)),
        pl.BlockSpec((tile, bw), lambda g, b, i: (0, g)),
        pl.BlockSpec((tile, HEAD), lambda g, b, i: (0, 0)),
        pl.BlockSpec((ONES_ROWS, tile), lambda g, b, i: (0, 0)),
        pl.BlockSpec((heads, tile, tile), lambda g, b, i: (g, 0, 0)),
        pl.BlockSpec((1, HEAD), lambda g, b, i: (0, 0)),
    ]
    args = [scal, jnp.asarray(_alibi_slopes_log2(n_heads), F32), qn, kn, vt,
            _attn_key_feat(n_heads, tile, mask_pad=not with_meta), query_feat, _ones_rows(tile),
            _attn_diag_bias(n_heads, tile), out_g.reshape(1, HEAD).astype(F32)]
    if with_meta:
        in_specs += [
            pl.BlockSpec((1, N_META, bw), lambda g, b, i: (0, 0, g)),
            pl.BlockSpec((1, heads, 1, HEAD, N_META), lambda g, b, i: (0, g, 0, 0, 0)),
            pl.BlockSpec((N_META, bw), lambda g, b, i: (0, g)),
            pl.BlockSpec((ONES_ROWS, N_META), lambda g, b, i: (0, 0)),
        ]
        args += [meta[0][:, PAD:], meta[1][..., PAD:],
                 _attn_key_feat(n_heads, META_ROWS, mask_pad=False)[PAD:], _ones_rows(N_META)]
    return pl.pallas_call(
        functools.partial(_attn_kernel, tile=tile, heads=heads, with_meta=with_meta, mask_pad=not with_meta),
        grid=(n_heads // heads, batch, nq),
        in_specs=in_specs,
        out_specs=pl.BlockSpec((1, tile, bw), lambda g, b, i: (b, i, g)),
        out_shape=jax.ShapeDtypeStruct((batch, length, width), BF16),
        scratch_shapes=[pltpu.VMEM((heads, HEAD + ONES_ROWS, 2 * tile), F32),
                        pltpu.VMEM((heads, 2, tile, 2 * tile), F32)],
        compiler_params=_params(("parallel", "parallel", "parallel")),
        name="diff_attention",
    )(*args)


def _hgrn_constants():
    c = HGRN_CHUNK
    t = np.arange(c)[:, None]
    r = np.arange(c)[None, :]
    sums = []
    pair = [np.eye(c)]
    for level in range(1, HGRN_LEVELS + 1):
        size = 2 ** level
        mid = (t // size) * size + size // 2 - 1
        upper = (t % size) >= size // 2
        sums.append(np.where(upper, (r > mid) & (r <= t), (r > t) & (r <= mid)))
        same = (t // size) == (r // size)
        pair.append(same & upper & ((r % size) < size // 2))
    sums.append(r <= t)
    sums.append(r > t)
    sums = np.concatenate(sums, axis=0).astype(np.float32)
    return (jnp.asarray(np.concatenate([sums, sums], axis=1), BF16),
            jnp.asarray(np.stack(pair).astype(np.float32)))


def _hgrn_kernel(*refs, heads, chunks, with_init, emit_state):
    refs = list(refs)
    q_ref, f_ref, v_ref, g_ref, lb_ref, og_ref, sums_ref, pair_ref = refs[:8]
    init_ref = refs[8] if with_init else None
    o_ref = refs[8 + with_init]
    state_out_ref = refs[9 + with_init] if emit_state else None
    state_ref = refs[-1]
    c = HGRN_CHUNK

    @pl.when(pl.program_id(2) == 0)
    def _():
        state_ref[...] = init_ref[...] if with_init else jnp.zeros_like(state_ref)

    def chunk_body(n):
        rows = slice(n * c, (n + 1) * c)
        lb_all = lb_ref[...]
        sig_all = jax.nn.sigmoid(f_ref[0, rows, :].astype(F32))
        logf = jnp.log2(lb_all + (1.0 - lb_all) * sig_all)
        k_all = ((1.0 - lb_all) * (1.0 - sig_all)).astype(BF16)
        hi = logf.astype(BF16)
        lo = (logf - hi.astype(F32)).astype(BF16)
        decay_all = jnp.exp2(jnp.dot(sums_ref[...], jnp.concatenate([hi, lo], axis=0),
                                     preferred_element_type=F32))
        for h in range(heads):
            cols = slice(h * HEAD, (h + 1) * HEAD)
            q = q_ref[0, rows, cols].astype(BF16)
            v = v_ref[0, rows, cols].astype(BF16)
            gate = g_ref[0, rows, cols].astype(F32)
            k = k_all[:, cols]
            decay = decay_all[:, cols]

            a = pair_ref[0] * lax.dot_general(q, k, (((1,), (1,)), ((), ())), preferred_element_type=F32)
            for level in range(1, HGRN_LEVELS + 1):
                d = decay[(level - 1) * c:level * c].astype(BF16)
                a = a + pair_ref[level] * lax.dot_general(q * d, k * d, (((1,), (1,)), ((), ())),
                                                          preferred_element_type=F32)
            from_start = decay[7 * c:8 * c]
            to_end = decay[8 * c:9 * c].astype(BF16)
            state_t = state_ref[h]
            o = jnp.dot(a.astype(BF16), v, preferred_element_type=F32)
            o = o + lax.dot_general(q * from_start.astype(BF16), state_t.astype(BF16),
                                    (((1,), (1,)), ((), ())), preferred_element_type=F32)
            state_ref[h] = state_t * from_start[c - 1:c, :] + lax.dot_general(
                v, k * to_end, (((0,), (0,)), ((), ())), preferred_element_type=F32)

            y = o * lax.rsqrt(jnp.mean(o * o, axis=-1, keepdims=True) + EPS) * og_ref[...]
            o_ref[0, rows, cols] = (y * (gate * jax.nn.sigmoid(gate))).astype(o_ref.dtype)

    for n in range(chunks):
        chunk_body(n)

    if emit_state:
        @pl.when(pl.program_id(2) == pl.num_programs(2) - 1)
        def _():
            state_out_ref[0] = state_ref[...]


def hgrn2(proj3, lb, out_g, width, init_state=None, emit_state=False):
    batch, length, _ = proj3.shape
    n_heads = width // HEAD
    heads = _pick(n_heads, (4, 2, 1))
    tile = _pick(length, (1024, 512, 384, 256, 128))
    groups = n_heads // heads
    bw = heads * HEAD
    sums, pair = _hgrn_constants()
    with_init = init_state is not None

    def seg_spec(seg):
        return pl.BlockSpec((1, tile, bw), lambda b, hg, t: (b, t, seg * groups + hg))

    in_specs = [
        seg_spec(0), seg_spec(1), seg_spec(2), seg_spec(3),
        pl.BlockSpec((1, bw), lambda b, hg, t: (0, hg)),
        pl.BlockSpec((1, HEAD), lambda b, hg, t: (0, 0)),
        pl.BlockSpec(sums.shape, lambda b, hg, t: (0, 0)),
        pl.BlockSpec(pair.shape, lambda b, hg, t: (0, 0, 0)),
    ]
    args = [proj3, proj3, proj3, proj3, lb.reshape(1, width).astype(F32),
            out_g.reshape(1, HEAD).astype(F32), sums, pair]
    if with_init:
        in_specs.append(pl.BlockSpec((heads, HEAD, HEAD), lambda b, hg, t: (hg, 0, 0)))
        args.append(init_state)
    out_specs = pl.BlockSpec((1, tile, bw), lambda b, hg, t: (b, t, hg))
    out_shape = jax.ShapeDtypeStruct((batch, length, width), BF16)
    if emit_state:
        out_specs = [out_specs, pl.BlockSpec((1, heads, HEAD, HEAD), lambda b, hg, t: (b, hg, 0, 0))]
        out_shape = [out_shape, jax.ShapeDtypeStruct((batch, n_heads, HEAD, HEAD), F32)]
    return pl.pallas_call(
        functools.partial(_hgrn_kernel, heads=heads, chunks=tile // HGRN_CHUNK, with_init=with_init,
                          emit_state=emit_state),
        grid=(batch, groups, length // tile),
        in_specs=in_specs,
        out_specs=out_specs,
        out_shape=out_shape,
        scratch_shapes=[pltpu.VMEM((heads, HEAD, HEAD), F32)],
        compiler_params=_params(("parallel", "parallel", "arbitrary")),
        name="hgrn2",
    )(*args)


def kernel(x, meta_tokens, norm1_g, w_in, hgrn_lb_raw, hgrn_out_g, q_norm_g, k_norm_g, diff_lambda,
           diff_out_g, w_out, norm2_g, w_mlp_up, w_mlp_down):
    batch, seq, d_model = x.shape
    depth = w_in.shape[0]
    width = d_model // 2
    tile = _pick(seq, (512, 384, 256, 128))
    h = x.reshape(batch * seq, d_model)
    hm = jnp.concatenate([jnp.zeros((PAD, d_model), x.dtype), meta_tokens.astype(x.dtype)], axis=0)

    lb_all = jnp.cumsum(jax.nn.softmax(hgrn_lb_raw.astype(F32), axis=0), axis=0)
    lb_all = lb_all - lb_all[0:1]
    q_scale = LOG2E / math.sqrt(HALF)

    for layer in range(depth):
        keep_meta = layer + 1 < depth
        q_gain = (q_norm_g[layer].reshape(1, HEAD) * q_scale).astype(F32)
        k_gain = k_norm_g[layer].reshape(1, HEAD).astype(F32)
        lp = diff_lambda[layer].astype(F32)
        lam_init = 0.8 - 0.6 * math.exp(-0.3 * layer)
        lam = jnp.exp(jnp.sum(lp[0] * lp[1])) - jnp.exp(jnp.sum(lp[2] * lp[3])) + lam_init

        proj, projm = matmul([rmsnorm(h, norm1_g[layer])], w_in, layer, BF16,
                             meta_xs=[rmsnorm(hm, norm1_g[layer])], single_buffer_rows=True)
        proj3 = proj.reshape(batch, seq, -1)
        projm3 = projm.reshape(1, META_ROWS, -1)
        o_am, state = hgrn2(projm3, lb_all[layer], hgrn_out_g[layer], width, emit_state=True)
        o_a = hgrn2(proj3, lb_all[layer], hgrn_out_g[layer], width, init_state=state[0])
        qn_m, kn_m, vt_m = qkv_prep(projm3, q_gain, k_gain, width, META_ROWS)
        qn, kn, vt = qkv_prep(proj3, q_gain, k_gain, width, tile)
        o_b = diff_attention(qn, kn, vt, lam, lam_init, diff_out_g[layer], tile, meta=(kn_m, vt_m))
        mixed = [o_a.reshape(batch * seq, width), o_b.reshape(batch * seq, width)]
        if keep_meta:
            o_bm = diff_attention(qn_m, kn_m, vt_m, lam, lam_init, diff_out_g[layer], META_ROWS)
            h, hm = matmul(mixed, w_out, layer, F32, epilogue="residual", residual=h,
                           meta_xs=[o_am.reshape(META_ROWS, width), o_bm.reshape(META_ROWS, width)],
                           meta_residual=hm)
            z, zm = matmul([rmsnorm(h, norm2_g[layer])], w_mlp_up, layer, BF16, epilogue="relu2",
                           meta_xs=[rmsnorm(hm, norm2_g[layer])], single_buffer_rows=True)
            h, hm = matmul_residual_kgrid(z, w_mlp_down, layer, h, meta_x=zm, meta_residual=hm)
        else:
            h = matmul(mixed, w_out, layer, F32, epilogue="residual", residual=h)
            z = matmul([rmsnorm(h, norm2_g[layer])], w_mlp_up, layer, BF16, epilogue="relu2",
                       single_buffer_rows=True)
            h = matmul_residual_kgrid(z, w_mlp_down, layer, h)
    return h.reshape(batch, seq, d_model)
```

```python
import functools
import math

import numpy as np
import jax
import jax.numpy as jnp
from jax import lax
from jax.experimental import pallas as pl
from jax.experimental.pallas import tpu as pltpu

N_META = 16
ATTN_CHUNK = 64
HEAD = 128
HALF = HEAD // 2
META_ROWS = 128
PAD = META_ROWS - N_META
HGRN_CHUNK = 128
HGRN_LEVELS = 7
ONES_ROWS = 16
ATTN_HEADS_PER_STEP = 4
EPS = 1e-6
MASK_VALUE = -1e30
LOG2E = math.log2(math.e)
V7X_VMEM_LIMIT_BYTES = 56 * 1024 * 1024
DENSE_WINDOW_BUDGET_BYTES = 50 * 1024 * 1024

F32 = jnp.float32
BF16 = jnp.bfloat16


def _params(semantics):
    return pltpu.CompilerParams(dimension_semantics=semantics, vmem_limit_bytes=V7X_VMEM_LIMIT_BYTES)


def _pick(n, candidates):
    for c in candidates:
        if n % c == 0:
            return c
    raise ValueError(f"no tile for {n} among {candidates}")


def _rmsnorm_kernel(x_ref, g_ref, o_ref):
    x = x_ref[...]
    ms = jnp.mean(x * x, axis=-1, keepdims=True)
    o_ref[...] = (x * lax.rsqrt(ms + EPS) * g_ref[...]).astype(o_ref.dtype)


def rmsnorm(x, g):
    m, d = x.shape
    bm = _pick(m, (512, 256, 128))
    return pl.pallas_call(
        _rmsnorm_kernel,
        grid=(m // bm,),
        in_specs=[pl.BlockSpec((bm, d), lambda i: (i, 0)), pl.BlockSpec((1, d), lambda i: (0, 0))],
        out_specs=pl.BlockSpec((bm, d), lambda i: (i, 0)),
        out_shape=jax.ShapeDtypeStruct((m, d), BF16),
        compiler_params=_params(("parallel",)),
        name="rmsnorm",
    )(x, g.reshape(1, d).astype(F32))


def _epilogue(acc, epilogue, r_ref):
    if epilogue == "relu2":
        acc = jnp.square(jnp.maximum(acc, 0.0))
    if epilogue == "residual":
        acc = acc + r_ref[...]
    return acc


def _matmul_kernel(*refs, n_parts, epilogue, with_meta):
    refs = list(refs)
    take = lambda n: [refs.pop(0) for _ in range(n)]
    xs, ws = take(n_parts), take(n_parts)
    r_ref = refs.pop(0) if epilogue == "residual" else None
    xms = take(n_parts) if with_meta else []
    rm_ref = refs.pop(0) if with_meta and epilogue == "residual" else None
    o_ref = refs.pop(0)

    def product(x_refs, res_ref):
        acc = jnp.dot(x_refs[0][...], ws[0][...].astype(BF16), preferred_element_type=F32)
        for x_ref, w_ref in zip(x_refs[1:], ws[1:]):
            acc = acc + jnp.dot(x_ref[...], w_ref[...].astype(BF16), preferred_element_type=F32)
        return _epilogue(acc, epilogue, res_ref)

    o_ref[...] = product(xs, r_ref).astype(o_ref.dtype)
    if with_meta:
        om_ref = refs.pop(0)

        @pl.when(pl.program_id(0) == 0)
        def _():
            om_ref[...] = product(xms, rm_ref).astype(om_ref.dtype)

        @pl.when(pl.program_id(0) != 0)
        def _():
            om_ref[...] = jnp.zeros_like(om_ref)


def _meta_col(i, j):
    return jnp.where(i == 0, j, 0)


def _meta_out(n, bn, dtype):
    return (pl.BlockSpec((META_ROWS, bn), lambda i, j, *_: (0, jnp.where(i == 0, j, n // bn))),
            jax.ShapeDtypeStruct((META_ROWS, n + bn), dtype))


def matmul(xs, w_stack, layer, out_dtype, epilogue=None, residual=None, meta_xs=None, meta_residual=None,
           single_buffer_rows=False):
    m = xs[0].shape[0]
    n = w_stack.shape[2]
    d_in = sum(x.shape[1] for x in xs)
    with_meta = meta_xs is not None
    bn = _pick(n, (512, 256, 128))

    row_buffers = 1 if single_buffer_rows else 2

    def vmem_bytes(bm):
        per_step = d_in * bn * 4 + bm * bn * jnp.dtype(out_dtype).itemsize
        if epilogue == "residual":
            per_step += bm * bn * 4
        return 2 * per_step + row_buffers * bm * d_in * 2

    bm = next(c for c in (2048, 1024, 512, 256, 128)
              if m % c == 0 and vmem_bytes(c) <= DENSE_WINDOW_BUDGET_BYTES)
    row_mode = dict(pipeline_mode=pl.Buffered(1)) if single_buffer_rows else {}
    in_specs = [pl.BlockSpec((bm, x.shape[1]), lambda i, j: (i, 0), **row_mode) for x in xs]
    row_block = 0
    for x in xs:
        kp = x.shape[1]
        in_specs.append(pl.BlockSpec((None, kp, bn), functools.partial(
            lambda i, j, rb: (layer, rb, j), rb=row_block // kp)))
        row_block += kp
    args = list(xs) + [w_stack] * len(xs)
    if epilogue == "residual":
        in_specs.append(pl.BlockSpec((bm, bn), lambda i, j: (i, j)))
        args.append(residual)
    out_specs = pl.BlockSpec((bm, bn), lambda i, j: (i, j))
    out_shape = jax.ShapeDtypeStruct((m, n), out_dtype)
    if with_meta:
        in_specs += [pl.BlockSpec((META_ROWS, x.shape[1]), lambda i, j: (0, 0)) for x in meta_xs]
        args += list(meta_xs)
        if epilogue == "residual":
            in_specs.append(pl.BlockSpec((META_ROWS, bn), lambda i, j: (0, _meta_col(i, j))))
            args.append(meta_residual)
        mspec, mshape = _meta_out(n, bn, out_dtype)
        out_specs, out_shape = [out_specs, mspec], [out_shape, mshape]
    out = pl.pallas_call(
        functools.partial(_matmul_kernel, n_parts=len(xs), epilogue=epilogue, with_meta=with_meta),
        grid=(m // bm, n // bn),
        in_specs=in_specs,
        out_specs=out_specs,
        out_shape=out_shape,
        compiler_params=_params(("arbitrary", "arbitrary") if with_meta else ("parallel", "parallel")),
        name="matmul_" + (epilogue or "plain"),
    )(*args)
    return (out[0], out[1][:, :n]) if with_meta else out


def _matmul_kgrid_kernel(*refs, with_meta):
    if with_meta:
        x_ref, w_ref, r_ref, xm_ref, rm_ref, o_ref, om_ref = refs
    else:
        x_ref, w_ref, r_ref, o_ref = refs
    i, k = pl.program_id(0), pl.program_id(2)

    @pl.when(k == 0)
    def _():
        o_ref[...] = r_ref[...]
        if with_meta:
            om_ref[...] = jnp.where(i == 0, rm_ref[...], 0.0)

    o_ref[...] += jnp.dot(x_ref[...], w_ref[...].astype(BF16), preferred_element_type=F32)

    if with_meta:
        @pl.when(i == 0)
        def _():
            om_ref[...] += jnp.dot(xm_ref[...], w_ref[...].astype(BF16), preferred_element_type=F32)


def matmul_residual_kgrid(x, w_stack, layer, residual, meta_x=None, meta_residual=None):
    m, kdim = x.shape
    n = w_stack.shape[2]
    with_meta = meta_x is not None
    bm = _pick(m, (1024, 512, 256, 128))
    bn = _pick(n, (1024, 512, 256, 128))
    bk = _pick(kdim, (2048, 1024, 512, 256, 128))
    tile = pl.BlockSpec((bm, bn), lambda i, j, k: (i, j))
    in_specs = [
        pl.BlockSpec((bm, bk), lambda i, j, k: (i, k)),
        pl.BlockSpec((None, bk, bn), lambda i, j, k: (layer, k, j)),
        tile,
    ]
    args = [x, w_stack, residual]
    out_specs, out_shape = tile, jax.ShapeDtypeStruct((m, n), F32)
    if with_meta:
        in_specs += [pl.BlockSpec((META_ROWS, bk), lambda i, j, k: (0, _meta_col(i, k))),
                     pl.BlockSpec((META_ROWS, bn), lambda i, j, k: (0, _meta_col(i, j)))]
        args += [meta_x, meta_residual]
        mspec, mshape = _meta_out(n, bn, F32)
        out_specs, out_shape = [out_specs, mspec], [out_shape, mshape]
    out = pl.pallas_call(
        functools.partial(_matmul_kgrid_kernel, with_meta=with_meta),
        grid=(m // bm, n // bn, kdim // bk),
        in_specs=in_specs,
        out_specs=out_specs,
        out_shape=out_shape,
        compiler_params=_params(("arbitrary",) * 3 if with_meta else ("parallel", "parallel", "arbitrary")),
        name="matmul_residual_kgrid",
    )(*args)
    return (out[0], out[1][:, :n]) if with_meta else out


def _qkv_prep_kernel(q_ref, k_ref, v_ref, gq_ref, gk_ref, pool_ref, qo_ref, ko_ref, vt_ref):
    pool = pool_ref[...]
    width = q_ref.shape[2]
    for h in range(width // HEAD):
        cols = slice(h * HEAD, (h + 1) * HEAD)
        for src, g_ref, dst in ((q_ref, gq_ref, qo_ref), (k_ref, gk_ref, ko_ref)):
            x = src[0, :, cols].astype(F32)
            ms = jnp.dot((x * x).astype(BF16), pool, preferred_element_type=F32)
            dst[0, :, cols] = (x * lax.rsqrt(ms + EPS) * g_ref[...]).astype(dst.dtype)
        vt_ref[0, h, 0] = v_ref[0, :, cols].astype(F32).T.astype(vt_ref.dtype)


def qkv_prep(proj3, q_gain, k_gain, width, tile):
    batch, length, _ = proj3.shape
    n_heads = width // HEAD
    nblk = length // tile
    pool = np.kron(np.eye(2), np.full((HALF, HALF), 1.0 / HALF)).astype(np.float32)
    qk = jax.ShapeDtypeStruct((batch, length, width), BF16)
    vt = jax.ShapeDtypeStruct((batch, n_heads, nblk, HEAD, tile), BF16)
    gspec = pl.BlockSpec((1, HEAD), lambda b, i: (0, 0))
    qkspec = pl.BlockSpec((1, tile, width), lambda b, i: (b, i, 0))
    return pl.pallas_call(
        _qkv_prep_kernel,
        grid=(batch, nblk),
        in_specs=[
            pl.BlockSpec((1, tile, width), lambda b, i: (b, i, 4)),
            pl.BlockSpec((1, tile, width), lambda b, i: (b, i, 5)),
            pl.BlockSpec((1, tile, width), lambda b, i: (b, i, 6)),
            gspec, gspec,
            pl.BlockSpec((HEAD, HEAD), lambda b, i: (0, 0)),
        ],
        out_specs=[qkspec, qkspec,
                   pl.BlockSpec((1, n_heads, 1, HEAD, tile), lambda b, i: (b, 0, i, 0, 0))],
        out_shape=[qk, qk, vt],
        compiler_params=_params(("parallel", "parallel")),
        name="qkv_prep",
    )(proj3, proj3, proj3, q_gain, k_gain, jnp.asarray(pool, BF16))


def _alibi_slopes_log2(n_heads):
    return np.exp2(-8.0 * np.arange(1, n_heads + 1, dtype=np.float64) / n_heads) * LOG2E


def _attn_key_feat(n_heads, rows, mask_pad):
    ki = np.arange(rows)
    feat = (_alibi_slopes_log2(n_heads)[None, :] * ki[:, None]).astype(np.float32)
    if mask_pad:
        feat = np.where((ki < PAD)[:, None], np.float32(MASK_VALUE), feat)
    hi = feat.astype(BF16)
    lo = np.where(feat <= MASK_VALUE, np.float32(0.0), feat - hi.astype(np.float32)).astype(BF16)
    key_feat = np.zeros((rows, n_heads, HEAD), BF16)
    key_feat[..., 0] = hi
    key_feat[..., 1] = lo
    return jnp.asarray(key_feat.reshape(rows, n_heads * HEAD))


def _attn_diag_bias(n_heads, tile):
    slopes = jnp.asarray(_alibi_slopes_log2(n_heads), F32)
    ki = jnp.arange(tile, dtype=jnp.int32)
    kk, qq = ki[:, None], ki[None, :]
    allowed = (kk // ATTN_CHUNK) <= (qq // ATTN_CHUNK)
    rel = (-jnp.abs(qq - kk) + qq - kk).astype(F32)
    return jnp.where(allowed[None], slopes[:, None, None] * rel[None], MASK_VALUE)


def _ones_rows(cols):
    return jnp.broadcast_to((jnp.arange(ONES_ROWS) == 0).astype(BF16)[:, None], (ONES_ROWS, cols))


def _attn_kernel(*refs, tile, heads, with_meta, mask_pad):
    refs = list(refs)
    scal_ref, slope_ref, q_ref, k_ref, vt_ref, kfeat_ref, qfeat_ref, ones_ref, dbias_ref, g_ref = refs[:10]
    if with_meta:
        km_ref, vtm_ref, kfeatm_ref, onesm_ref = refs[10:14]
    o_ref, acc_ref, s_ref = refs[-3:]
    i = pl.program_id(2)
    lam = scal_ref[0]
    out_scale = scal_ref[1]
    qfeat = qfeat_ref[...]
    ones = ones_ref[...]
    lane = lax.broadcasted_iota(jnp.int32, (tile, HEAD), 1)

    def head_cols(h):
        return slice(h * HEAD, (h + 1) * HEAD)

    qcat = []
    for h in range(heads):
        q = q_ref[0, :, head_cols(h)]
        qcat.append(jnp.concatenate(
            [jnp.concatenate([jnp.where(keep, q, jnp.zeros_like(q)), qfeat], axis=1)
             for keep in (lane < HALF, lane >= HALF)], axis=0))
    slope = [slope_ref[pl.program_id(0) * heads + h] for h in range(heads)]
    block_shift = [s * float(tile) for s in slope]

    def scores(kb, feat, h):
        kaug = jnp.concatenate([kb, feat], axis=1)
        return lax.dot_general(kaug, qcat[h], (((1,), (1,)), ((), ())), preferred_element_type=F32)

    def produce(h, j, slot, diag=False):
        kb = k_ref[0, pl.ds(pl.multiple_of(j * tile, tile), tile), head_cols(h)]
        s = scores(kb, kfeat_ref[:, head_cols(h)], h)
        if diag:
            s = s + jnp.concatenate([dbias_ref[h]] * 2, axis=1)
        s_ref[h, slot] = s
        return jnp.max(s, axis=0, keepdims=True)

    def consume(h, j, slot, col_max, m_old):
        vaug = jnp.concatenate([vt_ref[0, h, j], ones], axis=0)
        shift = block_shift[h] * (j - i).astype(F32)
        m_new = jnp.maximum(m_old, col_max + shift)
        p = jnp.exp2(s_ref[h, slot] - (m_new - shift)).astype(BF16)
        acc_ref[h] = acc_ref[h] * jnp.exp2(m_old - m_new) + jnp.dot(vaug, p, preferred_element_type=F32)
        return m_new

    def start(h):
        if not with_meta:
            acc_ref[h] = jnp.zeros(acc_ref.shape[1:], F32)
            return jnp.full((1, 2 * tile), MASK_VALUE, F32)
        s = scores(km_ref[0, :, head_cols(h)], kfeatm_ref[:, head_cols(h)], h)
        shift = -(slope[h] * float(META_ROWS) + block_shift[h] * i.astype(F32))
        m_new = jnp.max(s, axis=0, keepdims=True) + shift
        p = jnp.exp2(s - (m_new - shift)).astype(BF16)
        vaug = jnp.concatenate([vtm_ref[0, h, 0], onesm_ref[...]], axis=0)
        acc_ref[h] = jnp.dot(vaug, p, preferred_element_type=F32)
        return m_new

    init = tuple((start(h), produce(h, i, 0, diag=True)) for h in range(heads))

    def pipe(t, slot, carry):
        prev = jnp.where(t == 0, i, t - 1)
        return tuple((consume(h, prev, slot, carry[h][1], carry[h][0]), produce(h, t, 1 - slot))
                     for h in range(heads))

    carry = lax.fori_loop(0, i // 2, lambda n, c: pipe(2 * n + 1, 1, pipe(2 * n, 0, c)), init)

    def odd_tail(c):
        c = pipe(i - 1, 0, c)
        for h in range(heads):
            consume(h, i - 1, 1, c[h][1], c[h][0])
        return 0

    def even_tail(c):
        for h in range(heads):
            consume(h, jnp.where(i == 0, i, i - 1), 0, c[h][1], c[h][0])
        return 0

    lax.cond(i % 2 == 1, odd_tail, even_tail, carry)

    row = lax.broadcasted_iota(jnp.int32, (tile, HEAD), 0) + i * tile
    for h in range(heads):
        def normalized(c):
            cols = slice(c * tile, (c + 1) * tile)
            return acc_ref[h, :HEAD, cols] / acc_ref[h, HEAD:HEAD + 1, cols]

        o = normalized(0) - lam * normalized(1)
        o = o * lax.rsqrt(jnp.mean(o * o, axis=0, keepdims=True) + EPS)
        ot = o.T * (g_ref[...] * out_scale)
        if mask_pad:
            ot = jnp.where(row >= PAD, ot, 0.0)
        o_ref[0, :, head_cols(h)] = ot.astype(o_ref.dtype)


def diff_attention(qn, kn, vt, lam, lam_init, out_g, tile, meta=None):
    batch, length, width = qn.shape
    n_heads = width // HEAD
    heads = _pick(n_heads, (ATTN_HEADS_PER_STEP, 1))
    nq = length // tile
    with_meta = meta is not None
    scal = jnp.stack([lam.astype(F32), jnp.asarray(1.0 - lam_init, F32)])
    query_feat = jnp.broadcast_to((jnp.arange(HEAD) < 2).astype(BF16)[None, :], (tile, HEAD))
    smem = pl.BlockSpec(memory_space=pltpu.SMEM)
    bw = heads * HEAD
    in_specs = [
        smem, smem,
        pl.BlockSpec((1, tile, bw), lambda g, b, i: (b, i, g)),
        pl.BlockSpec((1, length, bw), lambda g, b, i: (b, 0, g)),
        pl.BlockSpec((1, heads, nq, HEAD, tile), lambda g, b, i: (b, g, 0, 0, 0)),
        pl.BlockSpec((tile, bw), lambda g, b, i: (0, g)),
        pl.BlockSpec((tile, HEAD), lambda g, b, i: (0, 0)),
        pl.BlockSpec((ONES_ROWS, tile), lambda g, b, i: (0, 0)),
        pl.BlockSpec((heads, tile, tile), lambda g, b, i: (g, 0, 0)),
        pl.BlockSpec((1, HEAD), lambda g, b, i: (0, 0)),
    ]
    args = [scal, jnp.asarray(_alibi_slopes_log2(n_heads), F32), qn, kn, vt,
            _attn_key_feat(n_heads, tile, mask_pad=not with_meta), query_feat, _ones_rows(tile),
            _attn_diag_bias(n_heads, tile), out_g.reshape(1, HEAD).astype(F32)]
    if with_meta:
        in_specs += [
            pl.BlockSpec((1, N_META, bw), lambda g, b, i: (0, 0, g)),
            pl.BlockSpec((1, heads, 1, HEAD, N_META), lambda g, b, i: (0, g, 0, 0, 0)),
            pl.BlockSpec((N_META, bw), lambda g, b, i: (0, g)),
            pl.BlockSpec((ONES_ROWS, N_META), lambda g, b, i: (0, 0)),
        ]
        args += [meta[0][:, PAD:], meta[1][..., PAD:],
                 _attn_key_feat(n_heads, META_ROWS, mask_pad=False)[PAD:], _ones_rows(N_META)]
    return pl.pallas_call(
        functools.partial(_attn_kernel, tile=tile, heads=heads, with_meta=with_meta, mask_pad=not with_meta),
        grid=(n_heads // heads, batch, nq),
        in_specs=in_specs,
        out_specs=pl.BlockSpec((1, tile, bw), lambda g, b, i: (b, i, g)),
        out_shape=jax.ShapeDtypeStruct((batch, length, width), BF16),
        scratch_shapes=[pltpu.VMEM((heads, HEAD + ONES_ROWS, 2 * tile), F32),
                        pltpu.VMEM((heads, 2, tile, 2 * tile), F32)],
        compiler_params=_params(("parallel", "parallel", "parallel")),
        name="diff_attention",
    )(*args)


def _hgrn_constants():
    c = HGRN_CHUNK
    t = np.arange(c)[:, None]
    r = np.arange(c)[None, :]
    sums = []
    pair = [np.eye(c)]
    for level in range(1, HGRN_LEVELS + 1):
        size = 2 ** level
        mid = (t // size) * size + size // 2 - 1
        upper = (t % size) >= size // 2
        sums.append(np.where(upper, (r > mid) & (r <= t), (r > t) & (r <= mid)))
        same = (t // size) == (r // size)
        pair.append(same & upper & ((r % size) < size // 2))
    sums.append(r <= t)
    sums.append(r > t)
    sums = np.concatenate(sums, axis=0).astype(np.float32)
    return (jnp.asarray(np.concatenate([sums, sums], axis=1), BF16),
            jnp.asarray(np.stack(pair).astype(np.float32)))


def _hgrn_kernel(*refs, heads, chunks, with_init, emit_state):
    refs = list(refs)
    q_ref, f_ref, v_ref, g_ref, lb_ref, og_ref, sums_ref, pair_ref = refs[:8]
    init_ref = refs[8] if with_init else None
    o_ref = refs[8 + with_init]
    state_out_ref = refs[9 + with_init] if emit_state else None
    state_ref = refs[-1]
    c = HGRN_CHUNK

    @pl.when(pl.program_id(2) == 0)
    def _():
        state_ref[...] = init_ref[...] if with_init else jnp.zeros_like(state_ref)

    def chunk_body(n):
        rows = slice(n * c, (n + 1) * c)
        lb_all = lb_ref[...]
        sig_all = jax.nn.sigmoid(f_ref[0, rows, :].astype(F32))
        logf = jnp.log2(lb_all + (1.0 - lb_all) * sig_all)
        k_all = ((1.0 - lb_all) * (1.0 - sig_all)).astype(BF16)
        hi = logf.astype(BF16)
        lo = (logf - hi.astype(F32)).astype(BF16)
        decay_all = jnp.exp2(jnp.dot(sums_ref[...], jnp.concatenate([hi, lo], axis=0),
                                     preferred_element_type=F32))
        for h in range(heads):
            cols = slice(h * HEAD, (h + 1) * HEAD)
            q = q_ref[0, rows, cols].astype(BF16)
            v = v_ref[0, rows, cols].astype(BF16)
            gate = g_ref[0, rows, cols].astype(F32)
            k = k_all[:, cols]
            decay = decay_all[:, cols]

            a = pair_ref[0] * lax.dot_general(q, k, (((1,), (1,)), ((), ())), preferred_element_type=F32)
            for level in range(1, HGRN_LEVELS + 1):
                d = decay[(level - 1) * c:level * c].astype(BF16)
                a = a + pair_ref[level] * lax.dot_general(q * d, k * d, (((1,), (1,)), ((), ())),
                                                          preferred_element_type=F32)
            from_start = decay[7 * c:8 * c]
            to_end = decay[8 * c:9 * c].astype(BF16)
            state_t = state_ref[h]
            o = jnp.dot(a.astype(BF16), v, preferred_element_type=F32)
            o = o + lax.dot_general(q * from_start.astype(BF16), state_t.astype(BF16),
                                    (((1,), (1,)), ((), ())), preferred_element_type=F32)
            state_ref[h] = state_t * from_start[c - 1:c, :] + lax.dot_general(
                v, k * to_end, (((0,), (0,)), ((), ())), preferred_element_type=F32)

            y = o * lax.rsqrt(jnp.mean(o * o, axis=-1, keepdims=True) + EPS) * og_ref[...]
            o_ref[0, rows, cols] = (y * (gate * jax.nn.sigmoid(gate))).astype(o_ref.dtype)

    for n in range(chunks):
        chunk_body(n)

    if emit_state:
        @pl.when(pl.program_id(2) == pl.num_programs(2) - 1)
        def _():
            state_out_ref[0] = state_ref[...]


def hgrn2(proj3, lb, out_g, width, init_state=None, emit_state=False):
    batch, length, _ = proj3.shape
    n_heads = width // HEAD
    heads = _pick(n_heads, (4, 2, 1))
    tile = _pick(length, (1024, 512, 384, 256, 128))
    groups = n_heads // heads
    bw = heads * HEAD
    sums, pair = _hgrn_constants()
    with_init = init_state is not None

    def seg_spec(seg):
        return pl.BlockSpec((1, tile, bw), lambda b, hg, t: (b, t, seg * groups + hg))

    in_specs = [
        seg_spec(0), seg_spec(1), seg_spec(2), seg_spec(3),
        pl.BlockSpec((1, bw), lambda b, hg, t: (0, hg)),
        pl.BlockSpec((1, HEAD), lambda b, hg, t: (0, 0)),
        pl.BlockSpec(sums.shape, lambda b, hg, t: (0, 0)),
        pl.BlockSpec(pair.shape, lambda b, hg, t: (0, 0, 0)),
    ]
    args = [proj3, proj3, proj3, proj3, lb.reshape(1, width).astype(F32),
            out_g.reshape(1, HEAD).astype(F32), sums, pair]
    if with_init:
        in_specs.append(pl.BlockSpec((heads, HEAD, HEAD), lambda b, hg, t: (hg, 0, 0)))
        args.append(init_state)
    out_specs = pl.BlockSpec((1, tile, bw), lambda b, hg, t: (b, t, hg))
    out_shape = jax.ShapeDtypeStruct((batch, length, width), BF16)
    if emit_state:
        out_specs = [out_specs, pl.BlockSpec((1, heads, HEAD, HEAD), lambda b, hg, t: (b, hg, 0, 0))]
        out_shape = [out_shape, jax.ShapeDtypeStruct((batch, n_heads, HEAD, HEAD), F32)]
    return pl.pallas_call(
        functools.partial(_hgrn_kernel, heads=heads, chunks=tile // HGRN_CHUNK, with_init=with_init,
                          emit_state=emit_state),
        grid=(batch, groups, length // tile),
        in_specs=in_specs,
        out_specs=out_specs,
        out_shape=out_shape,
        scratch_shapes=[pltpu.VMEM((heads, HEAD, HEAD), F32)],
        compiler_params=_params(("parallel", "parallel", "arbitrary")),
        name="hgrn2",
    )(*args)


def kernel(x, meta_tokens, norm1_g, w_in, hgrn_lb_raw, hgrn_out_g, q_norm_g, k_norm_g, diff_lambda,
           diff_out_g, w_out, norm2_g, w_mlp_up, w_mlp_down):
    batch, seq, d_model = x.shape
    depth = w_in.shape[0]
    width = d_model // 2
    tile = _pick(seq, (512, 384, 256, 128))
    h = x.reshape(batch * seq, d_model)
    hm = jnp.concatenate([jnp.zeros((PAD, d_model), x.dtype), meta_tokens.astype(x.dtype)], axis=0)

    lb_all = jnp.cumsum(jax.nn.softmax(hgrn_lb_raw.astype(F32), axis=0), axis=0)
    lb_all = lb_all - lb_all[0:1]
    q_scale = LOG2E / math.sqrt(HALF)

    for layer in range(depth):
        keep_meta = layer + 1 < depth
        q_gain = (q_norm_g[layer].reshape(1, HEAD) * q_scale).astype(F32)
        k_gain = k_norm_g[layer].reshape(1, HEAD).astype(F32)
        lp = diff_lambda[layer].astype(F32)
        lam_init = 0.8 - 0.6 * math.exp(-0.3 * layer)
        lam = jnp.exp(jnp.sum(lp[0] * lp[1])) - jnp.exp(jnp.sum(lp[2] * lp[3])) + lam_init

        proj, projm = matmul([rmsnorm(h, norm1_g[layer])], w_in, layer, BF16,
                             meta_xs=[rmsnorm(hm, norm1_g[layer])], single_buffer_rows=True)
        proj3 = proj.reshape(batch, seq, -1)
        projm3 = projm.reshape(1, META_ROWS, -1)
        o_am, state = hgrn2(projm3, lb_all[layer], hgrn_out_g[layer], width, emit_state=True)
        o_a = hgrn2(proj3, lb_all[layer], hgrn_out_g[layer], width, init_state=state[0])
        qn_m, kn_m, vt_m = qkv_prep(projm3, q_gain, k_gain, width, META_ROWS)
        qn, kn, vt = qkv_prep(proj3, q_gain, k_gain, width, tile)
        o_b = diff_attention(qn, kn, vt, lam, lam_init, diff_out_g[layer], tile, meta=(kn_m, vt_m))
        mixed = [o_a.reshape(batch * seq, width), o_b.reshape(batch * seq, width)]
        if keep_meta:
            o_bm = diff_attention(qn_m, kn_m, vt_m, lam, lam_init, diff_out_g[layer], META_ROWS)
            h, hm = matmul(mixed, w_out, layer, F32, epilogue="residual", residual=h,
                           meta_xs=[o_am.reshape(META_ROWS, width), o_bm.reshape(META_ROWS, width)],
                           meta_residual=hm)
            z, zm = matmul([rmsnorm(h, norm2_g[layer])], w_mlp_up, layer, BF16, epilogue="relu2",
                           meta_xs=[rmsnorm(hm, norm2_g[layer])], single_buffer_rows=True)
            h, hm = matmul_residual_kgrid(z, w_mlp_down, layer, h, meta_x=zm, meta_residual=hm)
        else:
            h = matmul(mixed, w_out, layer, F32, epilogue="residual", residual=h)
            z = matmul([rmsnorm(h, norm2_g[layer])], w_mlp_up, layer, BF16, epilogue="relu2",
                       single_buffer_rows=True)
            h = matmul_residual_kgrid(z, w_mlp_down, layer, h)
    return h.reshape(batch, seq, d_model)
```

```python
import functools
import math

import numpy as np
import jax
import jax.numpy as jnp
from jax import lax
from jax.experimental import pallas as pl
from jax.experimental.pallas import tpu as pltpu

N_META = 16
ATTN_CHUNK = 64
HEAD = 128
HALF = HEAD // 2
META_ROWS = 128
PAD = META_ROWS - N_META
HGRN_CHUNK = 128
HGRN_LEVELS = 7
ONES_ROWS = 16
ATTN_HEADS_PER_STEP = 4
EPS = 1e-6
MASK_VALUE = -1e30
LOG2E = math.log2(math.e)
V7X_VMEM_LIMIT_BYTES = 56 * 1024 * 1024
DENSE_WINDOW_BUDGET_BYTES = 50 * 1024 * 1024

F32 = jnp.float32
BF16 = jnp.bfloat16


def _params(semantics):
    return pltpu.CompilerParams(dimension_semantics=semantics, vmem_limit_bytes=V7X_VMEM_LIMIT_BYTES)


def _pick(n, candidates):
    for c in candidates:
        if n % c == 0:
            return c
    raise ValueError(f"no tile for {n} among {candidates}")


def _rmsnorm_kernel(x_ref, g_ref, o_ref):
    x = x_ref[...]
    ms = jnp.mean(x * x, axis=-1, keepdims=True)
    o_ref[...] = (x * lax.rsqrt(ms + EPS) * g_ref[...]).astype(o_ref.dtype)


def rmsnorm(x, g):
    m, d = x.shape
    bm = _pick(m, (512, 256, 128))
    return pl.pallas_call(
        _rmsnorm_kernel,
        grid=(m // bm,),
        in_specs=[pl.BlockSpec((bm, d), lambda i: (i, 0)), pl.BlockSpec((1, d), lambda i: (0, 0))],
        out_specs=pl.BlockSpec((bm, d), lambda i: (i, 0)),
        out_shape=jax.ShapeDtypeStruct((m, d), BF16),
        compiler_params=_params(("parallel",)),
        name="rmsnorm",
    )(x, g.reshape(1, d).astype(F32))


def _epilogue(acc, epilogue, r_ref):
    if epilogue == "relu2":
        acc = jnp.square(jnp.maximum(acc, 0.0))
    if epilogue == "residual":
        acc = acc + r_ref[...]
    return acc


def _matmul_kernel(*refs, n_parts, epilogue, with_meta):
    refs = list(refs)
    take = lambda n: [refs.pop(0) for _ in range(n)]
    xs, ws = take(n_parts), take(n_parts)
    r_ref = refs.pop(0) if epilogue == "residual" else None
    xms = take(n_parts) if with_meta else []
    rm_ref = refs.pop(0) if with_meta and epilogue == "residual" else None
    o_ref = refs.pop(0)

    def product(x_refs, res_ref):
        acc = jnp.dot(x_refs[0][...], ws[0][...].astype(BF16), preferred_element_type=F32)
        for x_ref, w_ref in zip(x_refs[1:], ws[1:]):
            acc = acc + jnp.dot(x_ref[...], w_ref[...].astype(BF16), preferred_element_type=F32)
        return _epilogue(acc, epilogue, res_ref)

    o_ref[...] = product(xs, r_ref).astype(o_ref.dtype)
    if with_meta:
        om_ref = refs.pop(0)

        @pl.when(pl.program_id(0) == 0)
        def _():
            om_ref[...] = product(xms, rm_ref).astype(om_ref.dtype)

        @pl.when(pl.program_id(0) != 0)
        def _():
            om_ref[...] = jnp.zeros_like(om_ref)


def _meta_col(i, j):
    return jnp.where(i == 0, j, 0)


def _meta_out(n, bn, dtype):
    return (pl.BlockSpec((META_ROWS, bn), lambda i, j, *_: (0, jnp.where(i == 0, j, n // bn))),
            jax.ShapeDtypeStruct((META_ROWS, n + bn), dtype))


def matmul(xs, w_stack, layer, out_dtype, epilogue=None, residual=None, meta_xs=None, meta_residual=None,
           single_buffer_rows=False):
    m = xs[0].shape[0]
    n = w_stack.shape[2]
    d_in = sum(x.shape[1] for x in xs)
    with_meta = meta_xs is not None
    bn = _pick(n, (512, 256, 128))

    row_buffers = 1 if single_buffer_rows else 2

    def vmem_bytes(bm):
        per_step = d_in * bn * 4 + bm * bn * jnp.dtype(out_dtype).itemsize
        if epilogue == "residual":
            per_step += bm * bn * 4
        return 2 * per_step + row_buffers * bm * d_in * 2

    bm = next(c for c in (2048, 1024, 512, 256, 128)
              if m % c == 0 and vmem_bytes(c) <= DENSE_WINDOW_BUDGET_BYTES)
    row_mode = dict(pipeline_mode=pl.Buffered(1)) if single_buffer_rows else {}
    in_specs = [pl.BlockSpec((bm, x.shape[1]), lambda i, j: (i, 0), **row_mode) for x in xs]
    row_block = 0
    for x in xs:
        kp = x.shape[1]
        in_specs.append(pl.BlockSpec((None, kp, bn), functools.partial(
            lambda i, j, rb: (layer, rb, j), rb=row_block // kp)))
        row_block += kp
    args = list(xs) + [w_stack] * len(xs)
    if epilogue == "residual":
        in_specs.append(pl.BlockSpec((bm, bn), lambda i, j: (i, j)))
        args.append(residual)
    out_specs = pl.BlockSpec((bm, bn), lambda i, j: (i, j))
    out_shape = jax.ShapeDtypeStruct((m, n), out_dtype)
    if with_meta:
        in_specs += [pl.BlockSpec((META_ROWS, x.shape[1]), lambda i, j: (0, 0)) for x in meta_xs]
        args += list(meta_xs)
        if epilogue == "residual":
            in_specs.append(pl.BlockSpec((META_ROWS, bn), lambda i, j: (0, _meta_col(i, j))))
            args.append(meta_residual)
        mspec, mshape = _meta_out(n, bn, out_dtype)
        out_specs, out_shape = [out_specs, mspec], [out_shape, mshape]
    out = pl.pallas_call(
        functools.partial(_matmul_kernel, n_parts=len(xs), epilogue=epilogue, with_meta=with_meta),
        grid=(m // bm, n // bn),
        in_specs=in_specs,
        out_specs=out_specs,
        out_shape=out_shape,
        compiler_params=_params(("arbitrary", "arbitrary") if with_meta else ("parallel", "parallel")),
        name="matmul_" + (epilogue or "plain"),
    )(*args)
    return (out[0], out[1][:, :n]) if with_meta else out


def _matmul_kgrid_kernel(*refs, with_meta):
    if with_meta:
        x_ref, w_ref, r_ref, xm_ref, rm_ref, o_ref, om_ref = refs
    else:
        x_ref, w_ref, r_ref, o_ref = refs
    i, k = pl.program_id(0), pl.program_id(2)

    @pl.when(k == 0)
    def _():
        o_ref[...] = r_ref[...]
        if with_meta:
            om_ref[...] = jnp.where(i == 0, rm_ref[...], 0.0)

    o_ref[...] += jnp.dot(x_ref[...], w_ref[...].astype(BF16), preferred_element_type=F32)

    if with_meta:
        @pl.when(i == 0)
        def _():
            om_ref[...] += jnp.dot(xm_ref[...], w_ref[...].astype(BF16), preferred_element_type=F32)


def matmul_residual_kgrid(x, w_stack, layer, residual, meta_x=None, meta_residual=None):
    m, kdim = x.shape
    n = w_stack.shape[2]
    with_meta = meta_x is not None
    bm = _pick(m, (1024, 512, 256, 128))
    bn = _pick(n, (1024, 512, 256, 128))
    bk = _pick(kdim, (2048, 1024, 512, 256, 128))
    tile = pl.BlockSpec((bm, bn), lambda i, j, k: (i, j))
    in_specs = [
        pl.BlockSpec((bm, bk), lambda i, j, k: (i, k)),
        pl.BlockSpec((None, bk, bn), lambda i, j, k: (layer, k, j)),
        tile,
    ]
    args = [x, w_stack, residual]
    out_specs, out_shape = tile, jax.ShapeDtypeStruct((m, n), F32)
    if with_meta:
        in_specs += [pl.BlockSpec((META_ROWS, bk), lambda i, j, k: (0, _meta_col(i, k))),
                     pl.BlockSpec((META_ROWS, bn), lambda i, j, k: (0, _meta_col(i, j)))]
        args += [meta_x, meta_residual]
        mspec, mshape = _meta_out(n, bn, F32)
        out_specs, out_shape = [out_specs, mspec], [out_shape, mshape]
    out = pl.pallas_call(
        functools.partial(_matmul_kgrid_kernel, with_meta=with_meta),
        grid=(m // bm, n // bn, kdim // bk),
        in_specs=in_specs,
        out_specs=out_specs,
        out_shape=out_shape,
        compiler_params=_params(("arbitrary",) * 3 if with_meta else ("parallel", "parallel", "arbitrary")),
        name="matmul_residual_kgrid",
    )(*args)
    return (out[0], out[1][:, :n]) if with_meta else out


def _qkv_prep_kernel(q_ref, k_ref, v_ref, gq_ref, gk_ref, pool_ref, qo_ref, ko_ref, vt_ref):
    pool = pool_ref[...]
    width = q_ref.shape[2]
    for h in range(width // HEAD):
        cols = slice(h * HEAD, (h + 1) * HEAD)
        for src, g_ref, dst in ((q_ref, gq_ref, qo_ref), (k_ref, gk_ref, ko_ref)):
            x = src[0, :, cols].astype(F32)
            ms = jnp.dot((x * x).astype(BF16), pool, preferred_element_type=F32)
            dst[0, :, cols] = (x * lax.rsqrt(ms + EPS) * g_ref[...]).astype(dst.dtype)
        vt_ref[0, h, 0] = v_ref[0, :, cols].astype(F32).T.astype(vt_ref.dtype)


def qkv_prep(proj3, q_gain, k_gain, width, tile):
    batch, length, _ = proj3.shape
    n_heads = width // HEAD
    nblk = length // tile
    pool = np.kron(np.eye(2), np.full((HALF, HALF), 1.0 / HALF)).astype(np.float32)
    qk = jax.ShapeDtypeStruct((batch, length, width), BF16)
    vt = jax.ShapeDtypeStruct((batch, n_heads, nblk, HEAD, tile), BF16)
    gspec = pl.BlockSpec((1, HEAD), lambda b, i: (0, 0))
    qkspec = pl.BlockSpec((1, tile, width), lambda b, i: (b, i, 0))
    return pl.pallas_call(
        _qkv_prep_kernel,
        grid=(batch, nblk),
        in_specs=[
            pl.BlockSpec((1, tile, width), lambda b, i: (b, i, 4)),
            pl.BlockSpec((1, tile, width), lambda b, i: (b, i, 5)),
            pl.BlockSpec((1, tile, width), lambda b, i: (b, i, 6)),
            gspec, gspec,
            pl.BlockSpec((HEAD, HEAD), lambda b, i: (0, 0)),
        ],
        out_specs=[qkspec, qkspec,
                   pl.BlockSpec((1, n_heads, 1, HEAD, tile), lambda b, i: (b, 0, i, 0, 0))],
        out_shape=[qk, qk, vt],
        compiler_params=_params(("parallel", "parallel")),
        name="qkv_prep",
    )(proj3, proj3, proj3, q_gain, k_gain, jnp.asarray(pool, BF16))


def _alibi_slopes_log2(n_heads):
    return np.exp2(-8.0 * np.arange(1, n_heads + 1, dtype=np.float64) / n_heads) * LOG2E


def _attn_key_feat(n_heads, rows, mask_pad):
    ki = np.arange(rows)
    feat = (_alibi_slopes_log2(n_heads)[None, :] * ki[:, None]).astype(np.float32)
    if mask_pad:
        feat = np.where((ki < PAD)[:, None], np.float32(MASK_VALUE), feat)
    hi = feat.astype(BF16)
    lo = np.where(feat <= MASK_VALUE, np.float32(0.0), feat - hi.astype(np.float32)).astype(BF16)
    key_feat = np.zeros((rows, n_heads, HEAD), BF16)
    key_feat[..., 0] = hi
    key_feat[..., 1] = lo
    return jnp.asarray(key_feat.reshape(rows, n_heads * HEAD))


def _attn_diag_bias(n_heads, tile):
    slopes = jnp.asarray(_alibi_slopes_log2(n_heads), F32)
    ki = jnp.arange(tile, dtype=jnp.int32)
    kk, qq = ki[:, None], ki[None, :]
    allowed = (kk // ATTN_CHUNK) <= (qq // ATTN_CHUNK)
    rel = (-jnp.abs(qq - kk) + qq - kk).astype(F32)
    return jnp.where(allowed[None], slopes[:, None, None] * rel[None], MASK_VALUE)


def _ones_rows(cols):
    return jnp.broadcast_to((jnp.arange(ONES_ROWS) == 0).astype(BF16)[:, None], (ONES_ROWS, cols))


def _attn_kernel(*refs, tile, heads, with_meta, mask_pad):
    refs = list(refs)
    scal_ref, slope_ref, q_ref, k_ref, vt_ref, kfeat_ref, qfeat_ref, ones_ref, dbias_ref, g_ref = refs[:10]
    if with_meta:
        km_ref, vtm_ref, kfeatm_ref, onesm_ref = refs[10:14]
    o_ref, acc_ref, s_ref = refs[-3:]
    i = pl.program_id(2)
    lam = scal_ref[0]
    out_scale = scal_ref[1]
    qfeat = qfeat_ref[...]
    ones = ones_ref[...]
    lane = lax.broadcasted_iota(jnp.int32, (tile, HEAD), 1)

    def head_cols(h):
        return slice(h * HEAD, (h + 1) * HEAD)

    qcat = []
    for h in range(heads):
        q = q_ref[0, :, head_cols(h)]
        qcat.append(jnp.concatenate(
            [jnp.concatenate([jnp.where(keep, q, jnp.zeros_like(q)), qfeat], axis=1)
             for keep in (lane < HALF, lane >= HALF)], axis=0))
    slope = [slope_ref[pl.program_id(0) * heads + h] for h in range(heads)]
    block_shift = [s * float(tile) for s in slope]

    def scores(kb, feat, h):
        kaug = jnp.concatenate([kb, feat], axis=1)
        return lax.dot_general(kaug, qcat[h], (((1,), (1,)), ((), ())), preferred_element_type=F32)

    def produce(h, j, slot, diag=False):
        kb = k_ref[0, pl.ds(pl.multiple_of(j * tile, tile), tile), head_cols(h)]
        s = scores(kb, kfeat_ref[:, head_cols(h)], h)
        if diag:
            s = s + jnp.concatenate([dbias_ref[h]] * 2, axis=1)
        s_ref[h, slot] = s
        return jnp.max(s, axis=0, keepdims=True)

    def consume(h, j, slot, col_max, m_old):
        vaug = jnp.concatenate([vt_ref[0, h, j], ones], axis=0)
        shift = block_shift[h] * (j - i).astype(F32)
        m_new = jnp.maximum(m_old, col_max + shift)
        p = jnp.exp2(s_ref[h, slot] - (m_new - shift)).astype(BF16)
        acc_ref[h] = acc_ref[h] * jnp.exp2(m_old - m_new) + jnp.dot(vaug, p, preferred_element_type=F32)
        return m_new

    def start(h):
        if not with_meta:
            acc_ref[h] = jnp.zeros(acc_ref.shape[1:], F32)
            return jnp.full((1, 2 * tile), MASK_VALUE, F32)
        s = scores(km_ref[0, :, head_cols(h)], kfeatm_ref[:, head_cols(h)], h)
        shift = -(slope[h] * float(META_ROWS) + block_shift[h] * i.astype(F32))
        m_new = jnp.max(s, axis=0, keepdims=True) + shift
        p = jnp.exp2(s - (m_new - shift)).astype(BF16)
        vaug = jnp.concatenate([vtm_ref[0, h, 0], onesm_ref[...]], axis=0)
        acc_ref[h] = jnp.dot(vaug, p, preferred_element_type=F32)
        return m_new

    init = tuple((start(h), produce(h, i, 0, diag=True)) for h in range(heads))

    def pipe(t, slot, carry):
        prev = jnp.where(t == 0, i, t - 1)
        return tuple((consume(h, prev, slot, carry[h][1], carry[h][0]), produce(h, t, 1 - slot))
                     for h in range(heads))

    carry = lax.fori_loop(0, i // 2, lambda n, c: pipe(2 * n + 1, 1, pipe(2 * n, 0, c)), init)

    def odd_tail(c):
        c = pipe(i - 1, 0, c)
        for h in range(heads):
            consume(h, i - 1, 1, c[h][1], c[h][0])
        return 0

    def even_tail(c):
        for h in range(heads):
            consume(h, jnp.where(i == 0, i, i - 1), 0, c[h][1], c[h][0])
        return 0

    lax.cond(i % 2 == 1, odd_tail, even_tail, carry)

    row = lax.broadcasted_iota(jnp.int32, (tile, HEAD), 0) + i * tile
    for h in range(heads):
        def normalized(c):
            cols = slice(c * tile, (c + 1) * tile)
            return acc_ref[h, :HEAD, cols] / acc_ref[h, HEAD:HEAD + 1, cols]

        o = normalized(0) - lam * normalized(1)
        o = o * lax.rsqrt(jnp.mean(o * o, axis=0, keepdims=True) + EPS)
        ot = o.T * (g_ref[...] * out_scale)
        if mask_pad:
            ot = jnp.where(row >= PAD, ot, 0.0)
        o_ref[0, :, head_cols(h)] = ot.astype(o_ref.dtype)


def diff_attention(qn, kn, vt, lam, lam_init, out_g, tile, meta=None):
    batch, length, width = qn.shape
    n_heads = width // HEAD
    heads = _pick(n_heads, (ATTN_HEADS_PER_STEP, 1))
    nq = length // tile
    with_meta = meta is not None
    scal = jnp.stack([lam.astype(F32), jnp.asarray(1.0 - lam_init, F32)])
    query_feat = jnp.broadcast_to((jnp.arange(HEAD) < 2).astype(BF16)[None, :], (tile, HEAD))
    smem = pl.BlockSpec(memory_space=pltpu.SMEM)
    bw = heads * HEAD
    in_specs = [
        smem, smem,
        pl.BlockSpec((1, tile, bw), lambda g, b, i: (b, i, g)),
        pl.BlockSpec((1, length, bw), lambda g, b, i: (b, 0, g)),
        pl.BlockSpec((1, heads, nq, HEAD, tile), lambda g, b, i: (b, g, 0, 0, 0)),
        pl.BlockSpec((tile, bw), lambda g, b, i: (0, g)),
        pl.BlockSpec((tile, HEAD), lambda g, b, i: (0, 0)),
        pl.BlockSpec((ONES_ROWS, tile), lambda g, b, i: (0, 0)),
        pl.BlockSpec((heads, tile, tile), lambda g, b, i: (g, 0, 0)),
        pl.BlockSpec((1, HEAD), lambda g, b, i: (0, 0)),
    ]
    args = [scal, jnp.asarray(_alibi_slopes_log2(n_heads), F32), qn, kn, vt,
            _attn_key_feat(n_heads, tile, mask_pad=not with_meta), query_feat, _ones_rows(tile),
            _attn_diag_bias(n_heads, tile), out_g.reshape(1, HEAD).astype(F32)]
    if with_meta:
        in_specs += [
            pl.BlockSpec((1, N_META, bw), lambda g, b, i: (0, 0, g)),
            pl.BlockSpec((1, heads, 1, HEAD, N_META), lambda g, b, i: (0, g, 0, 0, 0)),
            pl.BlockSpec((N_META, bw), lambda g, b, i: (0, g)),
            pl.BlockSpec((ONES_ROWS, N_META), lambda g, b, i: (0, 0)),
        ]
        args += [meta[0][:, PAD:], meta[1][..., PAD:],
                 _attn_key_feat(n_heads, META_ROWS, mask_pad=False)[PAD:], _ones_rows(N_META)]
    return pl.pallas_call(
        functools.partial(_attn_kernel, tile=tile, heads=heads, with_meta=with_meta, mask_pad=not with_meta),
        grid=(n_heads // heads, batch, nq),
        in_specs=in_specs,
        out_specs=pl.BlockSpec((1, tile, bw), lambda g, b, i: (b, i, g)),
        out_shape=jax.ShapeDtypeStruct((batch, length, width), BF16),
        scratch_shapes=[pltpu.VMEM((heads, HEAD + ONES_ROWS, 2 * tile), F32),
                        pltpu.VMEM((heads, 2, tile, 2 * tile), F32)],
        compiler_params=_params(("parallel", "parallel", "parallel")),
        name="diff_attention",
    )(*args)


def _hgrn_constants():
    c = HGRN_CHUNK
    t = np.arange(c)[:, None]
    r = np.arange(c)[None, :]
    sums = []
    pair = [np.eye(c)]
    for level in range(1, HGRN_LEVELS + 1):
        size = 2 ** level
        mid = (t // size) * size + size // 2 - 1
        upper = (t % size) >= size // 2
        sums.append(np.where(upper, (r > mid) & (r <= t), (r > t) & (r <= mid)))
        same = (t // size) == (r // size)
        pair.append(same & upper & ((r % size) < size // 2))
    sums.append(r <= t)
    sums.append(r > t)
    sums = np.concatenate(sums, axis=0).astype(np.float32)
    return (jnp.asarray(np.concatenate([sums, sums], axis=1), BF16),
            jnp.asarray(np.stack(pair).astype(np.float32)))


def _hgrn_kernel(*refs, heads, chunks, with_init, emit_state):
    refs = list(refs)
    q_ref, f_ref, v_ref, g_ref, lb_ref, og_ref, sums_ref, pair_ref = refs[:8]
    init_ref = refs[8] if with_init else None
    o_ref = refs[8 + with_init]
    state_out_ref = refs[9 + with_init] if emit_state else None
    state_ref = refs[-1]
    c = HGRN_CHUNK

    @pl.when(pl.program_id(2) == 0)
    def _():
        state_ref[...] = init_ref[...] if with_init else jnp.zeros_like(state_ref)

    def chunk_body(n):
        rows = slice(n * c, (n + 1) * c)
        lb_all = lb_ref[...]
        sig_all = jax.nn.sigmoid(f_ref[0, rows, :].astype(F32))
        logf = jnp.log2(lb_all + (1.0 - lb_all) * sig_all)
        k_all = ((1.0 - lb_all) * (1.0 - sig_all)).astype(BF16)
        hi = logf.astype(BF16)
        lo = (logf - hi.astype(F32)).astype(BF16)
        split = jnp.concatenate([hi, lo], axis=0)
        group = min(heads, 2)
        for h in range(heads):
            cols = slice(h * HEAD, (h + 1) * HEAD)
            if h % group == 0:
                gcols = slice(h * HEAD, (h + group) * HEAD)
                decay_group = jnp.exp2(jnp.dot(sums_ref[...], split[:, gcols], preferred_element_type=F32))
            q = q_ref[0, rows, cols].astype(BF16)
            v = v_ref[0, rows, cols].astype(BF16)
            gate = g_ref[0, rows, cols].astype(F32)
            k = k_all[:, cols]
            decay = decay_group[:, (h % group) * HEAD:(h % group + 1) * HEAD]

            a = pair_ref[0] * lax.dot_general(q, k, (((1,), (1,)), ((), ())), preferred_element_type=F32)
            for level in range(1, HGRN_LEVELS + 1):
                d = decay[(level - 1) * c:level * c].astype(BF16)
                a = a + pair_ref[level] * lax.dot_general(q * d, k * d, (((1,), (1,)), ((), ())),
                                                          preferred_element_type=F32)
            from_start = decay[7 * c:8 * c]
            to_end = decay[8 * c:9 * c].astype(BF16)
            state_t = state_ref[h]
            o = jnp.dot(a.astype(BF16), v, preferred_element_type=F32)
            o = o + lax.dot_general(q * from_start.astype(BF16), state_t.astype(BF16),
                                    (((1,), (1,)), ((), ())), preferred_element_type=F32)
            state_ref[h] = state_t * from_start[c - 1:c, :] + lax.dot_general(
                v, k * to_end, (((0,), (0,)), ((), ())), preferred_element_type=F32)

            y = o * lax.rsqrt(jnp.mean(o * o, axis=-1, keepdims=True) + EPS) * og_ref[...]
            o_ref[0, rows, cols] = (y * (gate * jax.nn.sigmoid(gate))).astype(o_ref.dtype)

    for n in range(chunks):
        chunk_body(n)

    if emit_state:
        @pl.when(pl.program_id(2) == pl.num_programs(2) - 1)
        def _():
            state_out_ref[0] = state_ref[...]


def hgrn2(proj3, lb, out_g, width, init_state=None, emit_state=False):
    batch, length, _ = proj3.shape
    n_heads = width // HEAD
    heads = _pick(n_heads, (4, 2, 1))
    tile = _pick(length, (1024, 512, 384, 256, 128))
    groups = n_heads // heads
    bw = heads * HEAD
    sums, pair = _hgrn_constants()
    with_init = init_state is not None

    def seg_spec(seg):
        return pl.BlockSpec((1, tile, bw), lambda b, hg, t: (b, t, seg * groups + hg))

    in_specs = [
        seg_spec(0), seg_spec(1), seg_spec(2), seg_spec(3),
        pl.BlockSpec((1, bw), lambda b, hg, t: (0, hg)),
        pl.BlockSpec((1, HEAD), lambda b, hg, t: (0, 0)),
        pl.BlockSpec(sums.shape, lambda b, hg, t: (0, 0)),
        pl.BlockSpec(pair.shape, lambda b, hg, t: (0, 0, 0)),
    ]
    args = [proj3, proj3, proj3, proj3, lb.reshape(1, width).astype(F32),
            out_g.reshape(1, HEAD).astype(F32), sums, pair]
    if with_init:
        in_specs.append(pl.BlockSpec((heads, HEAD, HEAD), lambda b, hg, t: (hg, 0, 0)))
        args.append(init_state)
    out_specs = pl.BlockSpec((1, tile, bw), lambda b, hg, t: (b, t, hg))
    out_shape = jax.ShapeDtypeStruct((batch, length, width), BF16)
    if emit_state:
        out_specs = [out_specs, pl.BlockSpec((1, heads, HEAD, HEAD), lambda b, hg, t: (b, hg, 0, 0))]
        out_shape = [out_shape, jax.ShapeDtypeStruct((batch, n_heads, HEAD, HEAD), F32)]
    return pl.pallas_call(
        functools.partial(_hgrn_kernel, heads=heads, chunks=tile // HGRN_CHUNK, with_init=with_init,
                          emit_state=emit_state),
        grid=(batch, groups, length // tile),
        in_specs=in_specs,
        out_specs=out_specs,
        out_shape=out_shape,
        scratch_shapes=[pltpu.VMEM((heads, HEAD, HEAD), F32)],
        compiler_params=_params(("parallel", "parallel", "arbitrary")),
        name="hgrn2",
    )(*args)


def kernel(x, meta_tokens, norm1_g, w_in, hgrn_lb_raw, hgrn_out_g, q_norm_g, k_norm_g, diff_lambda,
           diff_out_g, w_out, norm2_g, w_mlp_up, w_mlp_down):
    batch, seq, d_model = x.shape
    depth = w_in.shape[0]
    width = d_model // 2
    tile = _pick(seq, (512, 384, 256, 128))
    h = x.reshape(batch * seq, d_model)
    hm = jnp.concatenate([jnp.zeros((PAD, d_model), x.dtype), meta_tokens.astype(x.dtype)], axis=0)

    lb_all = jnp.cumsum(jax.nn.softmax(hgrn_lb_raw.astype(F32), axis=0), axis=0)
    lb_all = lb_all - lb_all[0:1]
    q_scale = LOG2E / math.sqrt(HALF)

    for layer in range(depth):
        keep_meta = layer + 1 < depth
        q_gain = (q_norm_g[layer].reshape(1, HEAD) * q_scale).astype(F32)
        k_gain = k_norm_g[layer].reshape(1, HEAD).astype(F32)
        lp = diff_lambda[layer].astype(F32)
        lam_init = 0.8 - 0.6 * math.exp(-0.3 * layer)
        lam = jnp.exp(jnp.sum(lp[0] * lp[1])) - jnp.exp(jnp.sum(lp[2] * lp[3])) + lam_init

        proj, projm = matmul([rmsnorm(h, norm1_g[layer])], w_in, layer, BF16,
                             meta_xs=[rmsnorm(hm, norm1_g[layer])], single_buffer_rows=True)
        proj3 = proj.reshape(batch, seq, -1)
        projm3 = projm.reshape(1, META_ROWS, -1)
        o_am, state = hgrn2(projm3, lb_all[layer], hgrn_out_g[layer], width, emit_state=True)
        o_a = hgrn2(proj3, lb_all[layer], hgrn_out_g[layer], width, init_state=state[0])
        qn_m, kn_m, vt_m = qkv_prep(projm3, q_gain, k_gain, width, META_ROWS)
        qn, kn, vt = qkv_prep(proj3, q_gain, k_gain, width, tile)
        o_b = diff_attention(qn, kn, vt, lam, lam_init, diff_out_g[layer], tile, meta=(kn_m, vt_m))
        mixed = [o_a.reshape(batch * seq, width), o_b.reshape(batch * seq, width)]
        if keep_meta:
            o_bm = diff_attention(qn_m, kn_m, vt_m, lam, lam_init, diff_out_g[layer], META_ROWS)
            h, hm = matmul(mixed, w_out, layer, F32, epilogue="residual", residual=h,
                           meta_xs=[o_am.reshape(META_ROWS, width), o_bm.reshape(META_ROWS, width)],
                           meta_residual=hm)
            z, zm = matmul([rmsnorm(h, norm2_g[layer])], w_mlp_up, layer, BF16, epilogue="relu2",
                           meta_xs=[rmsnorm(hm, norm2_g[layer])], single_buffer_rows=True)
            h, hm = matmul_residual_kgrid(z, w_mlp_down, layer, h, meta_x=zm, meta_residual=hm)
        else:
            h = matmul(mixed, w_out, layer, F32, epilogue="residual", residual=h)
            z = matmul([rmsnorm(h, norm2_g[layer])], w_mlp_up, layer, BF16, epilogue="relu2",
                       single_buffer_rows=True)
            h = matmul_residual_kgrid(z, w_mlp_down, layer, h)
    return h.reshape(batch, seq, d_model)
```
